```python
import jax
import jax.numpy as jnp
from jax import lax
import numpy as np

D_MODEL = 4096
BATCH = 4
SEQ = 2048
DEPTH = 2
DEC_BATCH = 8
DEC_SEQ = 4
PAST_LEN = 16384
PAGE_SIZE = 128

HD = 128
H_A = 16
G_A = 2
R_A = H_A // G_A
CMP_LEN = 32
CMP_STRIDE = 16
SEL_LEN = 64
SEL_CHUNKS = SEL_LEN // CMP_STRIDE
N_SEL = 16
WINDOW = 512
H_B = 16
H_I = 32
D_I = 128
DSA_TOPK = 256
C_GROUPS = 4
POOL_WINDOWS = (2, 4, 8, 16)
C_WIDTH = 2048
C_GW = C_WIDTH // C_GROUPS
POOL_BUF = max(POOL_WINDOWS) - 1
H_M = 4
N_MEM = 256
N_BRANCH = 4
W_A = H_A * HD
W_B = H_B * HD
W_M = H_M * HD
W_BR = W_A + W_B + C_WIDTH + W_M
QBLK = 128
ROPE_THETA = 10000.0
EPS = 1e-6
SCALE = HD ** -0.5
NEG = -1e30
FORCE = 1e9
SPLIT_SIZES = (
    W_A,
    2 * G_A * HD,
    2 * G_A * HD,
    2 * G_A * HD,
    3 * H_A,
    W_A,
    W_B,
    2 * HD,
    H_I * D_I,
    D_I,
    H_I,
    W_B,
    C_WIDTH,
    C_WIDTH,
    W_M,
    W_M,
    N_BRANCH * D_MODEL,
)
N_IN = sum(SPLIT_SIZES)

kernel_name = 'hybrid_nsa_dsa_pool_memory_step'


def rmsnorm(x, g):
    xf = x.astype(jnp.float32)
    y = xf * lax.rsqrt(jnp.mean(xf * xf, axis=-1, keepdims=True) + EPS)
    return (y * g.astype(jnp.float32)).astype(x.dtype)


def rope(x, pos):
    half = x.shape[-1] // 2
    inv_freq = ROPE_THETA ** (-jnp.arange(half, dtype=jnp.float32) / half)
    ang = pos.astype(jnp.float32)[:, None] * inv_freq[None, :]
    cos = jnp.cos(ang)[:, None, :]
    sin = jnp.sin(ang)[:, None, :]
    xf = x.astype(jnp.float32)
    x1, x2 = xf[..., :half], xf[..., half:]
    return jnp.concatenate([x1 * cos - x2 * sin, x2 * cos + x1 * sin], axis=-1).astype(x.dtype)


def masked_softmax(s, mask):
    s = jnp.where(mask, s.astype(jnp.float32), NEG)
    return jnp.where(mask, jax.nn.softmax(s, axis=-1), 0.0)


def front(x, pos, norm_g, w_in, qk_g):
    B, T, _ = x.shape
    z = rmsnorm(x, norm_g) @ w_in
    (q_a, kv_c, kv_s, kv_w, g_a, s_a, q_b, kv_b, q_i, k_i, w_i, s_b,
     u_c, s_c, q_m, s_m, g_m) = jnp.split(z, np.cumsum(SPLIT_SIZES)[:-1].tolist(), axis=-1)
    q_n = rmsnorm(q_a.reshape(B, T, H_A, HD), qk_g[0])
    kv_s = kv_s.reshape(B, T, 2, G_A, HD)
    kv_w = kv_w.reshape(B, T, 2, G_A, HD)
    kv_b = kv_b.reshape(B, T, 2, HD)
    k_b = rope(rmsnorm(kv_b[:, :, 0], qk_g[5])[:, :, None], pos)[:, :, 0]
    return {
        'q_n': q_n,
        'q_r': rope(q_n, pos),
        'kv_cmp': kv_c.reshape(B, T, 2, G_A, HD),
        'kv_sel': jnp.stack([rope(rmsnorm(kv_s[:, :, 0], qk_g[2]), pos), kv_s[:, :, 1]], axis=2),
        'kv_win': jnp.stack([rope(rmsnorm(kv_w[:, :, 0], qk_g[3]), pos), kv_w[:, :, 1]], axis=2),
        'g_a': g_a.reshape(B, T, 3, H_A),
        'q_b': rope(rmsnorm(q_b.reshape(B, T, H_B, HD), qk_g[4]), pos),
        'kv_b': jnp.stack([k_b, kv_b[:, :, 1]], axis=2),
        'q_i': rope(q_i.reshape(B, T, H_I, D_I), pos),
        'k_i': rope(k_i[:, :, None], pos)[:, :, 0],
        'w_i': w_i,
        'u_c': u_c,
        'q_m': rmsnorm(q_m.reshape(B, T, H_M, HD), qk_g[6]),
        'silu': (s_a, s_b, s_c, s_m),
        'gates': g_m,
    }


def compress(rows, w_cmp, pe_cmp, g_k):
    B, L = rows.shape[:2]
    n_ch = L // CMP_STRIDE
    ch = rows[:, :n_ch * CMP_STRIDE].reshape(B, n_ch, CMP_STRIDE, 2, G_A, HD)
    ch = jnp.moveaxis(ch, 3, 0)
    pe = pe_cmp[:, None, None]
    c = (jnp.einsum('kbncgd,kgcde->kbnge', ch[:, :, :-1] + pe[:, :, :, :CMP_STRIDE], w_cmp[:, :, :CMP_STRIDE])
         + jnp.einsum('kbncgd,kgcde->kbnge', ch[:, :, 1:] + pe[:, :, :, CMP_STRIDE:], w_cmp[:, :, CMP_STRIDE:]))
    return rmsnorm(c[0], g_k), c[1]


def cmp_attend(q_n, pos_q, kc, vc):
    B, T = q_n.shape[:2]
    qg = q_n.reshape(B, T, G_A, R_A, HD)
    s = jnp.einsum('btgrd,bngd->btgrn', qg, kc) * SCALE
    blk_end = jnp.arange(kc.shape[1], dtype=jnp.int32) * CMP_STRIDE + (CMP_LEN - 1)
    mask = blk_end[None, :] <= pos_q[:, None]
    p = masked_softmax(s, mask[None, :, None, None, :])
    o = jnp.einsum('btgrn,bngd->btgrd', p.astype(vc.dtype), vc)
    return o.reshape(B, T, H_A, HD), p


def select_blocks(p_cmp, pos_q, n_blk):
    pg = p_cmp.sum(axis=3)
    zero = jnp.zeros(pg.shape[:-1] + (1,), pg.dtype)
    chunk = jnp.concatenate([pg, zero], -1) + jnp.concatenate([zero, pg], -1)
    chunk = jnp.pad(chunk, ((0, 0), (0, 0), (0, 0), (0, n_blk * SEL_CHUNKS - chunk.shape[-1])))
    score = chunk.reshape(chunk.shape[:-1] + (n_blk, SEL_CHUNKS)).sum(-1)
    j = jnp.arange(n_blk, dtype=jnp.int32)[None, :]
    cur = (pos_q // SEL_LEN)[:, None]
    forced = (j == 0) | (j == cur) | (j == cur - 1)
    valid = j <= cur
    score = jnp.where(forced[None, :, None], FORCE, jnp.where(valid[None, :, None], score, NEG))
    top, idx = lax.top_k(score, min(N_SEL, n_blk))
    return idx, top > 0.5 * NEG


def sel_attend(q_r, pos_q, kb, vb, idx, valid):
    B, T = q_r.shape[:2]
    n = idx.shape[-1] * SEL_LEN
    qg = q_r.reshape(B, T, G_A, R_A, HD)
    s = jnp.einsum('btgrd,btgkud->btgrku', qg, kb).reshape(B, T, G_A, R_A, n) * SCALE
    kpos = idx[..., None] * SEL_LEN + jnp.arange(SEL_LEN, dtype=jnp.int32)
    mask = (kpos <= pos_q[None, :, None, None, None]) & valid[..., None]
    p = masked_softmax(s, mask.reshape(B, T, G_A, 1, n))
    o = jnp.einsum('btgrn,btgnd->btgrd', p.astype(vb.dtype), vb.reshape(B, T, G_A, n, HD))
    return o.reshape(B, T, H_A, HD)


def win_attend(q_r, pos_q, kw, vw, pos_k):
    B, T = q_r.shape[:2]
    qg = q_r.reshape(B, T, G_A, R_A, HD)
    s = jnp.einsum('btgrd,bsgd->btgrs', qg, kw) * SCALE
    d = pos_q[:, None] - pos_k[None, :]
    mask = (d >= 0) & (d < WINDOW) & (pos_k[None, :] >= 0)
    p = masked_softmax(s, mask[None, :, None, None, :])
    o = jnp.einsum('btgrs,bsgd->btgrd', p.astype(vw.dtype), vw)
    return o.reshape(B, T, H_A, HD)


def nsa_combine(g_a, o_c, o_s, o_w):
    g = jax.nn.sigmoid(g_a.astype(jnp.float32)).astype(o_c.dtype)[..., None]
    o = g[:, :, 0] * o_c + g[:, :, 1] * o_s + g[:, :, 2] * o_w
    return o.reshape(o.shape[0], o.shape[1], W_A)


def dsa_select(q_i, w_i, k_i, pos_q, pos_k, k_top):
    logits = jnp.einsum('bthd,bsd->bths', q_i, k_i).astype(jnp.float32) * (D_I ** -0.5)
    score = jnp.einsum('bths,bth->bts', jax.nn.relu(logits), w_i.astype(jnp.float32)) * (H_I ** -0.5)
    score = jnp.where((pos_k[None, :] <= pos_q[:, None])[None], score, NEG)
    top, idx = lax.top_k(score, k_top)
    return idx, top > 0.5 * NEG


def dsa_attend(q_b, kvg, valid):
    B, T = q_b.shape[:2]
    s = jnp.einsum('bthd,btkd->bthk', q_b, kvg[..., 0, :]) * SCALE
    p = masked_softmax(s, valid[:, :, None, :])
    o = jnp.einsum('bthk,btkd->bthd', p.astype(kvg.dtype), kvg[..., 1, :])
    return o.reshape(B, T, W_B)


def pool_mix(u_ext, pos_q, w_pool, pool_scale):
    B, n_ext, _ = u_ext.shape
    T = n_ext - POOL_BUF
    uf = u_ext.astype(jnp.float32)
    cs = jnp.concatenate([jnp.zeros((B, 1, C_WIDTH), jnp.float32), jnp.cumsum(uf, axis=1)], axis=1)
    end = cs[:, POOL_BUF + 1:]
    means = []
    for gi, w in enumerate(POOL_WINDOWS):
        ch = slice(gi * C_GW, (gi + 1) * C_GW)
        start = cs[:, POOL_BUF + 1 - w: POOL_BUF + 1 - w + T, ch]
        cnt = jnp.minimum(pos_q + 1, w).astype(jnp.float32)[None, :, None]
        means.append((end[:, :, ch] - start) / cnt)
    d = jnp.concatenate(means, axis=-1) - uf[:, POOL_BUF:]
    y = jnp.einsum('btgc,gce->btge', d.reshape(B, T, C_GROUPS, C_GW), w_pool.astype(jnp.float32))
    return (y.reshape(B, T, C_WIDTH) * pool_scale.astype(jnp.float32)).astype(u_ext.dtype)


def mem_kv(mem, mem_norm_g, w_mem_kv, g_k):
    B, M, _ = mem.shape
    kv = (rmsnorm(mem, mem_norm_g) @ w_mem_kv).reshape(B, M, 2, H_M, HD)
    return jnp.stack([rmsnorm(kv[:, :, 0], g_k), kv[:, :, 1]], axis=2)


def mem_attend(q_m, kv):
    B, T = q_m.shape[:2]
    s = jnp.einsum('bthd,bmhd->bthm', q_m, kv[:, :, 0]) * SCALE
    p = jax.nn.softmax(s.astype(jnp.float32), axis=-1)
    o = jnp.einsum('bthm,bmhd->bthd', p.astype(kv.dtype), kv[:, :, 1])
    return o.reshape(B, T, W_M)


def merge(x, outs, silus, gates, w_br, w_out):
    B, T, _ = x.shape
    g = jax.nn.sigmoid(gates.astype(jnp.float32)).reshape(B, T, N_BRANCH, D_MODEL).astype(x.dtype)
    y = jnp.zeros_like(x)
    row = 0
    for b in range(N_BRANCH):
        width = outs[b].shape[-1]
        y = y + g[:, :, b] * ((outs[b] * jax.nn.silu(silus[b])) @ w_br[row:row + width])
        row += width
    return x + y @ w_out


def gather_pages(pool, l, page_table):
    rows = pool[l, page_table]
    return rows.reshape((page_table.shape[0], -1) + rows.shape[3:])


def gather_sel_sample(pool, l, page_table, new_rows, idx, past):
    Bd, T = new_rows.shape[:2]
    n_past_blk = past // SEL_LEN
    n_new_blk = -(-T // SEL_LEN)
    blk_per_page = PAGE_SIZE // SEL_LEN
    bi = jnp.arange(Bd)[:, None, None, None]
    gi = jnp.arange(G_A)[None, None, :, None]
    u = jnp.arange(SEL_LEN, dtype=jnp.int32)
    pb = jnp.minimum(idx, n_past_blk - 1)
    page = page_table[bi, pb // blk_per_page][..., None]
    off = ((pb % blk_per_page) * SEL_LEN)[..., None] + u
    past_rows = pool[l, page, off, :, gi[..., None], :]
    new_pad = jnp.pad(new_rows, ((0, 0), (0, n_new_blk * SEL_LEN - T), (0, 0), (0, 0), (0, 0)))
    new_pad = new_pad.reshape(Bd, n_new_blk, SEL_LEN, 2, G_A, HD)
    nb = jnp.clip(idx - n_past_blk, 0, n_new_blk - 1)
    new_g = new_pad[bi, nb, :, :, gi, :]
    res = jnp.where((idx < n_past_blk)[..., None, None, None], past_rows, new_g)
    return res[..., 0, :], res[..., 1, :]


def gather_rows_sample(pool, l, page_table, new_rows, idx, past):
    Bd, T = new_rows.shape[:2]
    bi = jnp.arange(Bd)[:, None, None]
    pi = jnp.minimum(idx, past - 1)
    past_rows = pool[l, page_table[bi, pi // PAGE_SIZE], pi % PAGE_SIZE]
    new_g = new_rows[bi, jnp.clip(idx - past, 0, T - 1)]
    return jnp.where((idx < past)[..., None, None], past_rows, new_g)


def nsa_prompt(q_n, q_r, kc, vc, kv_sel, kv_win):
    B, S = q_n.shape[:2]
    nqb = S // QBLK
    ns = S // SEL_LEN
    sel_blocks = kv_sel.reshape(B, ns, SEL_LEN, 2, G_A, HD).transpose(3, 0, 4, 1, 2, 5)
    win_pad = jnp.pad(kv_win, ((0, 0), (WINDOW, 0), (0, 0), (0, 0), (0, 0)))
    bi = jnp.arange(B)[:, None, None, None]
    gi = jnp.arange(G_A)[None, None, :, None]

    def block(args):
        i, qn, qr = args
        start = i * QBLK
        pos_q = start + jnp.arange(QBLK, dtype=jnp.int32)
        o_c, p_c = cmp_attend(qn, pos_q, kc, vc)
        idx, valid = select_blocks(p_c, pos_q, ns)
        o_s = sel_attend(qr, pos_q, sel_blocks[0][bi, gi, idx], sel_blocks[1][bi, gi, idx], idx, valid)
        win = lax.dynamic_slice_in_dim(win_pad, start, QBLK + WINDOW, axis=1)
        pos_k = start - WINDOW + jnp.arange(QBLK + WINDOW, dtype=jnp.int32)
        o_w = win_attend(qr, pos_q, win[:, :, 0], win[:, :, 1], pos_k)
        return o_c, o_s, o_w

    def to_blocks(t):
        return t.reshape(B, nqb, QBLK, H_A, HD).swapaxes(0, 1)

    o_c, o_s, o_w = lax.map(block, (jnp.arange(nqb, dtype=jnp.int32), to_blocks(q_n), to_blocks(q_r)))

    def from_blocks(t):
        return t.swapaxes(0, 1).reshape(B, S, H_A, HD)

    return from_blocks(o_c), from_blocks(o_s), from_blocks(o_w)


def dsa_prompt(q_b, kv_b, q_i, k_i, w_i):
    B, S = q_b.shape[:2]
    nqb = S // QBLK
    k_top = min(DSA_TOPK, S // 4)
    pos_k = jnp.arange(S, dtype=jnp.int32)
    bi = jnp.arange(B)[:, None, None]

    def block(args):
        i, qb, qi, wi = args
        pos_q = i * QBLK + jnp.arange(QBLK, dtype=jnp.int32)
        idx, valid = dsa_select(qi, wi, k_i, pos_q, pos_k, k_top)
        return dsa_attend(qb, kv_b[bi, idx], valid)

    def to_blocks(t):
        return t.reshape((B, nqb, QBLK) + t.shape[2:]).swapaxes(0, 1)

    o = lax.map(block, (jnp.arange(nqb, dtype=jnp.int32), to_blocks(q_b), to_blocks(q_i), to_blocks(w_i)))
    return o.swapaxes(0, 1).reshape(B, S, W_B)


def prompt_layer(x, mem, pos, norm_g, w_in, qk_g, w_cmp, pe_cmp, w_pool, pool_scale, mem_norm_g, w_mem_kv, w_br, w_out):
    S = x.shape[1]
    f = front(x, pos, norm_g, w_in, qk_g)
    kc, vc = compress(f['kv_cmp'], w_cmp, pe_cmp, qk_g[1])
    o_c, o_s, o_w = nsa_prompt(f['q_n'], f['q_r'], kc, vc, f['kv_sel'], f['kv_win'])
    o_a = nsa_combine(f['g_a'], o_c, o_s, o_w)
    o_b = dsa_prompt(f['q_b'], f['kv_b'], f['q_i'], f['k_i'], f['w_i'])
    u_ext = jnp.pad(f['u_c'], ((0, 0), (POOL_BUF, 0), (0, 0)))
    o_p = pool_mix(u_ext, pos, w_pool, pool_scale)
    kv_m = mem_kv(mem, mem_norm_g, w_mem_kv, qk_g[7])
    o_m = mem_attend(f['q_m'], kv_m)
    y = merge(x, (o_a, o_b, o_p, o_m), f['silu'], f['gates'], w_br, w_out)
    wb = min(WINDOW, S)
    return y, (f['kv_cmp'], f['kv_sel'], f['kv_b'], f['k_i'], f['kv_win'][:, S - wb:], f['u_c'][:, S - POOL_BUF:], kv_m)


def sample_layer(x, pos, l, page_table, cache_cmp, cache_sel, cache_dsa, cache_idx, win_buf, pool_buf, mem_cache,
                 norm_g, w_in, qk_g, w_cmp, pe_cmp, w_pool, pool_scale, w_br, w_out):
    T = x.shape[1]
    past = page_table.shape[1] * PAGE_SIZE
    L = past + T
    f = front(x, pos, norm_g, w_in, qk_g)
    cmp_rows = jnp.concatenate([gather_pages(cache_cmp, l, page_table), f['kv_cmp']], axis=1)
    kc, vc = compress(cmp_rows, w_cmp, pe_cmp, qk_g[1])
    o_c, p_c = cmp_attend(f['q_n'], pos, kc, vc)
    idx, valid = select_blocks(p_c, pos, -(-L // SEL_LEN))
    kg, vg = gather_sel_sample(cache_sel, l, page_table, f['kv_sel'], idx, past)
    o_s = sel_attend(f['q_r'], pos, kg, vg, idx, valid)
    wb = win_buf.shape[1]
    win_rows = jnp.concatenate([win_buf, f['kv_win']], axis=1)
    pos_k = past - wb + jnp.arange(wb + T, dtype=jnp.int32)
    o_w = win_attend(f['q_r'], pos, win_rows[:, :, 0], win_rows[:, :, 1], pos_k)
    o_a = nsa_combine(f['g_a'], o_c, o_s, o_w)
    k_i_all = jnp.concatenate([gather_pages(cache_idx, l, page_table), f['k_i']], axis=1)
    idx_b, valid_b = dsa_select(f['q_i'], f['w_i'], k_i_all, pos, jnp.arange(L, dtype=jnp.int32), min(DSA_TOPK, L // 4))
    kvg = gather_rows_sample(cache_dsa, l, page_table, f['kv_b'], idx_b, past)
    o_b = dsa_attend(f['q_b'], kvg, valid_b)
    u_ext = jnp.concatenate([pool_buf, f['u_c']], axis=1)
    o_p = pool_mix(u_ext, pos, w_pool, pool_scale)
    o_m = mem_attend(f['q_m'], mem_cache)
    y = merge(x, (o_a, o_b, o_p, o_m), f['silu'], f['gates'], w_br, w_out)
    return y, (f['kv_cmp'], f['kv_sel'], f['kv_b'], f['k_i'], win_rows[:, T:], u_ext[:, T:])


def setup_inputs(seed: int = 0) -> dict:
    key = jax.random.key(seed)
    ks = jax.random.split(key, 24)
    n_pages = PAST_LEN // PAGE_SIZE
    n_pool = (5 * DEC_BATCH * n_pages + 3) // 4
    wb = min(WINDOW, PAST_LEN)

    def nrm(k, shape, scale=1.0):
        return jax.random.normal(k, shape, jnp.float32) * scale

    def gain(k, shape):
        return 1.0 + nrm(k, shape, 0.02)

    page_table = jax.random.permutation(ks[10], n_pool)[:DEC_BATCH * n_pages].reshape(DEC_BATCH, n_pages).astype(jnp.int32)
    return {
        'x_prompt': nrm(ks[0], (BATCH, SEQ, D_MODEL)),
        'x_sample': nrm(ks[1], (DEC_BATCH, DEC_SEQ, D_MODEL)),
        'mem_prompt': nrm(ks[2], (BATCH, N_MEM, D_MODEL)),
        'cache_cmp': nrm(ks[3], (DEPTH, n_pool, PAGE_SIZE, 2, G_A, HD)),
        'cache_sel': nrm(ks[4], (DEPTH, n_pool, PAGE_SIZE, 2, G_A, HD)),
        'cache_dsa': nrm(ks[5], (DEPTH, n_pool, PAGE_SIZE, 2, HD)),
        'cache_idx': nrm(ks[6], (DEPTH, n_pool, PAGE_SIZE, D_I)),
        'state_win': nrm(ks[7], (DEPTH, DEC_BATCH, wb, 2, G_A, HD)),
        'state_pool': nrm(ks[8], (DEPTH, DEC_BATCH, POOL_BUF, C_WIDTH)),
        'cache_mem': nrm(ks[9], (DEPTH, DEC_BATCH, N_MEM, 2, H_M, HD)),
        'page_table': page_table,
        'norm_g': gain(ks[11], (DEPTH, D_MODEL)),
        'w_in': nrm(ks[12], (DEPTH, D_MODEL, N_IN), D_MODEL ** -0.5),
        'qk_g': gain(ks[13], (DEPTH, 8, HD)),
        'w_cmp': nrm(ks[14], (DEPTH, 2, G_A, CMP_LEN, HD, HD), (CMP_LEN * HD) ** -0.5),
        'pe_cmp': nrm(ks[15], (DEPTH, 2, CMP_LEN, G_A, HD), 0.1),
        'w_pool': nrm(ks[16], (DEPTH, C_GROUPS, C_GW, C_GW), C_GW ** -0.5),
        'pool_scale': gain(ks[17], (DEPTH, C_WIDTH)),
        'mem_norm_g': gain(ks[18], (DEPTH, D_MODEL)),
        'w_mem_kv': nrm(ks[19], (DEPTH, D_MODEL, 2 * W_M), D_MODEL ** -0.5),
        'w_br': nrm(ks[20], (DEPTH, W_BR, D_MODEL), W_A ** -0.5),
        'w_out': nrm(ks[21], (DEPTH, D_MODEL, D_MODEL), D_MODEL ** -0.5),
    }


def reference(x_prompt, x_sample, mem_prompt, cache_cmp, cache_sel, cache_dsa, cache_idx, state_win, state_pool,
              cache_mem, page_table, norm_g, w_in, qk_g, w_cmp, pe_cmp, w_pool, pool_scale, mem_norm_g, w_mem_kv,
              w_br, w_out):
    past = page_table.shape[1] * PAGE_SIZE
    pos_p = jnp.arange(x_prompt.shape[1], dtype=jnp.int32)
    pos_s = past + jnp.arange(x_sample.shape[1], dtype=jnp.int32)
    xp, xs = x_prompt, x_sample
    st_p, st_s = [], []
    for l in range(DEPTH):
        xp, sp = prompt_layer(xp, mem_prompt, pos_p, norm_g[l], w_in[l], qk_g[l], w_cmp[l], pe_cmp[l], w_pool[l],
                              pool_scale[l], mem_norm_g[l], w_mem_kv[l], w_br[l], w_out[l])
        xs, ss = sample_layer(xs, pos_s, l, page_table, cache_cmp, cache_sel, cache_dsa, cache_idx, state_win[l],
                              state_pool[l], cache_mem[l], norm_g[l], w_in[l], qk_g[l], w_cmp[l], pe_cmp[l],
                              w_pool[l], pool_scale[l], w_br[l], w_out[l])
        st_p.append(sp)
        st_s.append(ss)

    def stack(states, i):
        return jnp.stack([s[i] for s in states], axis=0)

    return (xp, xs,
            stack(st_p, 0), stack(st_s, 0),
            stack(st_p, 1), stack(st_s, 1),
            stack(st_p, 2), stack(st_s, 2),
            stack(st_p, 3), stack(st_s, 3),
            stack(st_p, 4), stack(st_s, 4),
            stack(st_p, 5), stack(st_s, 5),
            stack(st_p, 6))
```

```python
import functools

import numpy as np
import jax
import jax.numpy as jnp
from jax import lax
from jax.experimental import pallas as pl
from jax.experimental.pallas import tpu as pltpu

D_MODEL = 4096
DEPTH = 2
PAGE_SIZE = 128
HD = 128
H_A = 16
G_A = 2
R_A = H_A // G_A
CMP_LEN = 32
CMP_STRIDE = 16
SEL_LEN = 64
SEL_CHUNKS = SEL_LEN // CMP_STRIDE
N_SEL = 16
WINDOW = 512
H_B = 16
H_I = 32
D_I = 128
DSA_TOPK = 256
C_GROUPS = 4
POOL_WINDOWS = (2, 4, 8, 16)
C_WIDTH = 2048
C_GW = C_WIDTH // C_GROUPS
POOL_BUF = max(POOL_WINDOWS) - 1
H_M = 4
N_BRANCH = 4
W_A = H_A * HD
W_B = H_B * HD
W_M = H_M * HD
QBLK = 128
ROPE_THETA = 10000.0
EPS = 1e-6
SCALE = HD ** -0.5
NEG = -1e30
FORCE = 1e9

LANES = 128
SEG_NAMES = ('q_a', 'kv_c', 'kv_s', 'kv_w', 'g_a', 's_a', 'q_b', 'kv_b', 'q_i', 'k_i', 'w_i', 's_b',
             'u_c', 's_c', 'q_m', 's_m', 'g_m')
SEG_SIZES = (W_A, 2 * G_A * HD, 2 * G_A * HD, 2 * G_A * HD, 3 * H_A, W_A, W_B, 2 * HD, H_I * D_I, D_I, H_I, W_B,
             C_WIDTH, C_WIDTH, W_M, W_M, N_BRANCH * D_MODEL)
SEG_PAD = tuple(-(-s // LANES) * LANES for s in SEG_SIZES)
SEG_OFF = dict(zip(SEG_NAMES, np.concatenate([[0], np.cumsum(SEG_PAD)[:-1]]).tolist()))
SEG_LEN = dict(zip(SEG_NAMES, SEG_SIZES))
TN_FRONT = 512
N_FRONT = -(-sum(SEG_PAD) // TN_FRONT) * TN_FRONT
VMEM_LIMIT = 48 * 1024 * 1024


def _rmsnorm_cast_body(x_ref, g_ref, o_ref):
    x = x_ref[...]
    y = x * lax.rsqrt(jnp.mean(x * x, axis=-1, keepdims=True) + EPS)
    o_ref[...] = (y * g_ref[...]).astype(o_ref.dtype)


def rmsnorm_cast(x, g, tm):
    m, d = x.shape
    return pl.pallas_call(
        _rmsnorm_cast_body,
        grid=(m // tm,),
        in_specs=[pl.BlockSpec((tm, d), lambda i: (i, 0)), pl.BlockSpec((1, d), lambda i: (0, 0))],
        out_specs=pl.BlockSpec((tm, d), lambda i: (i, 0)),
        out_shape=jax.ShapeDtypeStruct((m, d), jnp.bfloat16),
        compiler_params=pltpu.CompilerParams(dimension_semantics=("parallel",), vmem_limit_bytes=VMEM_LIMIT),
        name="rmsnorm_cast",
    )(x, g.reshape(1, d))


def _matmul_body(a_ref, w_ref, o_ref):
    o_ref[...] = jnp.dot(a_ref[...], w_ref[...], preferred_element_type=jnp.float32)


def matmul(a, w, tm, tn):
    m, k = a.shape
    n = w.shape[1]
    return pl.pallas_call(
        _matmul_body,
        grid=(m // tm, n // tn),
        in_specs=[pl.BlockSpec((tm, k), lambda i, j: (i, 0)), pl.BlockSpec((k, tn), lambda i, j: (0, j))],
        out_specs=pl.BlockSpec((tm, tn), lambda i, j: (i, j)),
        out_shape=jax.ShapeDtypeStruct((m, n), jnp.float32),
        compiler_params=pltpu.CompilerParams(dimension_semantics=("parallel", "parallel"),
                                             vmem_limit_bytes=VMEM_LIMIT),
        name="matmul",
    )(a, w)


def _row_tile(m):
    return 1024 if m % 1024 == 0 else m


def dense(x2d, w_bf16):
    m = x2d.shape[0]
    n = w_bf16.shape[1]
    tn = 512 if n % 512 == 0 else n
    return matmul(x2d.astype(jnp.bfloat16), w_bf16, _row_tile(m), tn)


def prep_w_in(w):
    parts, off = [], 0
    for size, pad in zip(SEG_SIZES, SEG_PAD):
        p = w[:, off:off + size].astype(jnp.bfloat16)
        if pad > size:
            p = jnp.pad(p, ((0, 0), (0, pad - size)))
        parts.append(p)
        off += size
    tail = N_FRONT - sum(SEG_PAD)
    if tail:
        parts.append(jnp.zeros((w.shape[0], tail), jnp.bfloat16))
    return jnp.concatenate(parts, axis=1)


def seg(z, name):
    return z[..., SEG_OFF[name]:SEG_OFF[name] + SEG_LEN[name]]


def rmsnorm(x, g):
    y = x * lax.rsqrt(jnp.mean(x * x, axis=-1, keepdims=True) + EPS)
    return y * g


def rope(x, pos):
    half = x.shape[-1] // 2
    inv_freq = ROPE_THETA ** (-jnp.arange(half, dtype=jnp.float32) / half)
    ang = pos.astype(jnp.float32)[:, None] * inv_freq[None, :]
    cos = jnp.cos(ang)[:, None, :]
    sin = jnp.sin(ang)[:, None, :]
    x1, x2 = x[..., :half], x[..., half:]
    return jnp.concatenate([x1 * cos - x2 * sin, x2 * cos + x1 * sin], axis=-1)


def masked_softmax(s, mask):
    s = jnp.where(mask, s, NEG)
    return jnp.where(mask, jax.nn.softmax(s, axis=-1), 0.0)


def front(x, pos, norm_g, w_in_b, qk_g):
    B, T, _ = x.shape
    x2 = x.reshape(B * T, D_MODEL)
    h = rmsnorm_cast(x2, norm_g, 512 if (B * T) % 512 == 0 else B * T)
    z = matmul(h, w_in_b, _row_tile(B * T), TN_FRONT).reshape(B, T, N_FRONT)
    q_n = rmsnorm(seg(z, 'q_a').reshape(B, T, H_A, HD), qk_g[0])
    kv_s = seg(z, 'kv_s').reshape(B, T, 2, G_A, HD)
    kv_w = seg(z, 'kv_w').reshape(B, T, 2, G_A, HD)
    kv_b = seg(z, 'kv_b').reshape(B, T, 2, HD)
    k_b = rope(rmsnorm(kv_b[:, :, 0], qk_g[5])[:, :, None], pos)[:, :, 0]
    return {
        'q_n': q_n,
        'q_r': rope(q_n, pos),
        'kv_cmp': seg(z, 'kv_c').reshape(B, T, 2, G_A, HD),
        'kv_sel': jnp.stack([rope(rmsnorm(kv_s[:, :, 0], qk_g[2]), pos), kv_s[:, :, 1]], axis=2),
        'kv_win': jnp.stack([rope(rmsnorm(kv_w[:, :, 0], qk_g[3]), pos), kv_w[:, :, 1]], axis=2),
        'g_a': seg(z, 'g_a').reshape(B, T, 3, H_A),
        'q_b': rope(rmsnorm(seg(z, 'q_b').reshape(B, T, H_B, HD), qk_g[4]), pos),
        'kv_b': jnp.stack([k_b, kv_b[:, :, 1]], axis=2),
        'q_i': rope(seg(z, 'q_i').reshape(B, T, H_I, D_I), pos),
        'k_i': rope(seg(z, 'k_i')[:, :, None], pos)[:, :, 0],
        'w_i': seg(z, 'w_i'),
        'u_c': seg(z, 'u_c'),
        'q_m': rmsnorm(seg(z, 'q_m').reshape(B, T, H_M, HD), qk_g[6]),
        'silu': (seg(z, 's_a'), seg(z, 's_b'), seg(z, 's_c'), seg(z, 's_m')),
        'gates': seg(z, 'g_m'),
    }


def compress(rows, w_cmp, pe_cmp, g_k):
    B, L = rows.shape[:2]
    n_ch = L // CMP_STRIDE
    ch = rows[:, :n_ch * CMP_STRIDE].reshape(B, n_ch, CMP_STRIDE, 2, G_A, HD)
    ch = jnp.moveaxis(ch, 3, 0)
    pe = pe_cmp[:, None, None]
    c = (jnp.einsum('kbncgd,kgcde->kbnge', ch[:, :, :-1] + pe[:, :, :, :CMP_STRIDE], w_cmp[:, :, :CMP_STRIDE])
         + jnp.einsum('kbncgd,kgcde->kbnge', ch[:, :, 1:] + pe[:, :, :, CMP_STRIDE:], w_cmp[:, :, CMP_STRIDE:]))
    return rmsnorm(c[0], g_k), c[1]


def cmp_attend(q_n, pos_q, kc, vc):
    B, T = q_n.shape[:2]
    qg = q_n.reshape(B, T, G_A, R_A, HD)
    s = jnp.einsum('btgrd,bngd->btgrn', qg, kc) * SCALE
    blk_end = jnp.arange(kc.shape[1], dtype=jnp.int32) * CMP_STRIDE + (CMP_LEN - 1)
    mask = blk_end[None, :] <= pos_q[:, None]
    p = masked_softmax(s, mask[None, :, None, None, :])
    o = jnp.einsum('btgrn,bngd->btgrd', p, vc)
    return o.reshape(B, T, H_A, HD), p


def select_blocks(p_cmp, pos_q, n_blk):
    pg = p_cmp.sum(axis=3)
    zero = jnp.zeros(pg.shape[:-1] + (1,), pg.dtype)
    chunk = jnp.concatenate([pg, zero], -1) + jnp.concatenate([zero, pg], -1)
    chunk = jnp.pad(chunk, ((0, 0), (0, 0), (0, 0), (0, n_blk * SEL_CHUNKS - chunk.shape[-1])))
    score = chunk.reshape(chunk.shape[:-1] + (n_blk, SEL_CHUNKS)).sum(-1)
    j = jnp.arange(n_blk, dtype=jnp.int32)[None, :]
    cur = (pos_q // SEL_LEN)[:, None]
    forced = (j == 0) | (j == cur) | (j == cur - 1)
    valid = j <= cur
    score = jnp.where(forced[None, :, None], FORCE, jnp.where(valid[None, :, None], score, NEG))
    top, idx = lax.top_k(score, min(N_SEL, n_blk))
    return idx, top > 0.5 * NEG


def sel_attend(q_r, pos_q, kb, vb, idx, valid):
    B, T = q_r.shape[:2]
    n = idx.shape[-1] * SEL_LEN
    qg = q_r.reshape(B, T, G_A, R_A, HD)
    s = jnp.einsum('btgrd,btgkud->btgrku', qg, kb).reshape(B, T, G_A, R_A, n) * SCALE
    kpos = idx[..., None] * SEL_LEN + jnp.arange(SEL_LEN, dtype=jnp.int32)
    mask = (kpos <= pos_q[None, :, None, None, None]) & valid[..., None]
    p = masked_softmax(s, mask.reshape(B, T, G_A, 1, n))
    o = jnp.einsum('btgrn,btgnd->btgrd', p, vb.reshape(B, T, G_A, n, HD))
    return o.reshape(B, T, H_A, HD)


def win_attend(q_r, pos_q, kw, vw, pos_k):
    B, T = q_r.shape[:2]
    qg = q_r.reshape(B, T, G_A, R_A, HD)
    s = jnp.einsum('btgrd,bsgd->btgrs', qg, kw) * SCALE
    d = pos_q[:, None] - pos_k[None, :]
    mask = (d >= 0) & (d < WINDOW) & (pos_k[None, :] >= 0)
    p = masked_softmax(s, mask[None, :, None, None, :])
    o = jnp.einsum('btgrs,bsgd->btgrd', p, vw)
    return o.reshape(B, T, H_A, HD)


def nsa_combine(g_a, o_c, o_s, o_w):
    g = jax.nn.sigmoid(g_a)[..., None]
    o = g[:, :, 0] * o_c + g[:, :, 1] * o_s + g[:, :, 2] * o_w
    return o.reshape(o.shape[0], o.shape[1], W_A)


def dsa_select(q_i, w_i, k_i, pos_q, pos_k, k_top):
    logits = jnp.einsum('bthd,bsd->bths', q_i, k_i) * (D_I ** -0.5)
    score = jnp.einsum('bths,bth->bts', jax.nn.relu(logits), w_i) * (H_I ** -0.5)
    score = jnp.where((pos_k[None, :] <= pos_q[:, None])[None], score, NEG)
    top, idx = lax.top_k(score, k_top)
    return idx, top > 0.5 * NEG


def dsa_attend(q_b, kvg, valid):
    B, T = q_b.shape[:2]
    s = jnp.einsum('bthd,btkd->bthk', q_b, kvg[..., 0, :]) * SCALE
    p = masked_softmax(s, valid[:, :, None, :])
    o = jnp.einsum('bthk,btkd->bthd', p, kvg[..., 1, :])
    return o.reshape(B, T, W_B)


def pool_mix(u_ext, pos_q, w_pool, pool_scale):
    B, n_ext, _ = u_ext.shape
    T = n_ext - POOL_BUF
    cs = jnp.concatenate([jnp.zeros((B, 1, C_WIDTH), jnp.float32), jnp.cumsum(u_ext, axis=1)], axis=1)
    end = cs[:, POOL_BUF + 1:]
    means = []
    for gi, w in enumerate(POOL_WINDOWS):
        ch = slice(gi * C_GW, (gi + 1) * C_GW)
        start = cs[:, POOL_BUF + 1 - w: POOL_BUF + 1 - w + T, ch]
        cnt = jnp.minimum(pos_q + 1, w).astype(jnp.float32)[None, :, None]
        means.append((end[:, :, ch] - start) / cnt)
    d = jnp.concatenate(means, axis=-1) - u_ext[:, POOL_BUF:]
    y = jnp.einsum('btgc,gce->btge', d.reshape(B, T, C_GROUPS, C_GW), w_pool)
    return y.reshape(B, T, C_WIDTH) * pool_scale


def mem_kv(mem, mem_norm_g, w_mem_kv_b, g_k):
    B, M, _ = mem.shape
    h = rmsnorm_cast(mem.reshape(B * M, D_MODEL), mem_norm_g, 512)
    kv = matmul(h, w_mem_kv_b, _row_tile(B * M), 512).reshape(B, M, 2, H_M, HD)
    return jnp.stack([rmsnorm(kv[:, :, 0], g_k), kv[:, :, 1]], axis=2)


def mem_attend(q_m, kv):
    B, T = q_m.shape[:2]
    s = jnp.einsum('bthd,bmhd->bthm', q_m, kv[:, :, 0]) * SCALE
    p = jax.nn.softmax(s, axis=-1)
    o = jnp.einsum('bthm,bmhd->bthd', p, kv[:, :, 1])
    return o.reshape(B, T, W_M)


def merge(x, outs, silus, gates, w_br_b, w_out_b):
    B, T, _ = x.shape
    g = jax.nn.sigmoid(gates).reshape(B, T, N_BRANCH, D_MODEL)
    y = jnp.zeros_like(x)
    row = 0
    for b in range(N_BRANCH):
        width = outs[b].shape[-1]
        a = (outs[b] * jax.nn.silu(silus[b])).reshape(B * T, width)
        y = y + g[:, :, b] * dense(a, w_br_b[row:row + width]).reshape(B, T, D_MODEL)
        row += width
    return x + dense(y.reshape(B * T, D_MODEL), w_out_b).reshape(B, T, D_MODEL)


def gather_pages(pool, l, page_table):
    rows = pool[l, page_table]
    return rows.reshape((page_table.shape[0], -1) + rows.shape[3:])


def gather_sel_sample(pool, l, page_table, new_rows, idx, past):
    Bd, T = new_rows.shape[:2]
    n_past_blk = past // SEL_LEN
    n_new_blk = -(-T // SEL_LEN)
    blk_per_page = PAGE_SIZE // SEL_LEN
    bi = jnp.arange(Bd)[:, None, None, None]
    gi = jnp.arange(G_A)[None, None, :, None]
    u = jnp.arange(SEL_LEN, dtype=jnp.int32)
    pb = jnp.minimum(idx, n_past_blk - 1)
    page = page_table[bi, pb // blk_per_page][..., None]
    off = ((pb % blk_per_page) * SEL_LEN)[..., None] + u
    past_rows = pool[l, page, off, :, gi[..., None], :]
    new_pad = jnp.pad(new_rows, ((0, 0), (0, n_new_blk * SEL_LEN - T), (0, 0), (0, 0), (0, 0)))
    new_pad = new_pad.reshape(Bd, n_new_blk, SEL_LEN, 2, G_A, HD)
    nb = jnp.clip(idx - n_past_blk, 0, n_new_blk - 1)
    new_g = new_pad[bi, nb, :, :, gi, :]
    res = jnp.where((idx < n_past_blk)[..., None, None, None], past_rows, new_g)
    return res[..., 0, :], res[..., 1, :]


def gather_rows_sample(pool, l, page_table, new_rows, idx, past):
    Bd, T = new_rows.shape[:2]
    bi = jnp.arange(Bd)[:, None, None]
    pi = jnp.minimum(idx, past - 1)
    past_rows = pool[l, page_table[bi, pi // PAGE_SIZE], pi % PAGE_SIZE]
    new_g = new_rows[bi, jnp.clip(idx - past, 0, T - 1)]
    return jnp.where((idx < past)[..., None, None], past_rows, new_g)


def nsa_prompt(q_n, q_r, kc, vc, kv_sel, kv_win):
    B, S = q_n.shape[:2]
    nqb = S // QBLK
    ns = S // SEL_LEN
    sel_blocks = kv_sel.reshape(B, ns, SEL_LEN, 2, G_A, HD).transpose(3, 0, 4, 1, 2, 5)
    win_pad = jnp.pad(kv_win, ((0, 0), (WINDOW, 0), (0, 0), (0, 0), (0, 0)))
    bi = jnp.arange(B)[:, None, None, None]
    gi = jnp.arange(G_A)[None, None, :, None]

    def block(args):
        i, qn, qr = args
        start = i * QBLK
        pos_q = start + jnp.arange(QBLK, dtype=jnp.int32)
        o_c, p_c = cmp_attend(qn, pos_q, kc, vc)
        idx, valid = select_blocks(p_c, pos_q, ns)
        o_s = sel_attend(qr, pos_q, sel_blocks[0][bi, gi, idx], sel_blocks[1][bi, gi, idx], idx, valid)
        win = lax.dynamic_slice_in_dim(win_pad, start, QBLK + WINDOW, axis=1)
        pos_k = start - WINDOW + jnp.arange(QBLK + WINDOW, dtype=jnp.int32)
        o_w = win_attend(qr, pos_q, win[:, :, 0], win[:, :, 1], pos_k)
        return o_c, o_s, o_w

    def to_blocks(t):
        return t.reshape(B, nqb, QBLK, H_A, HD).swapaxes(0, 1)

    o_c, o_s, o_w = lax.map(block, (jnp.arange(nqb, dtype=jnp.int32), to_blocks(q_n), to_blocks(q_r)))

    def from_blocks(t):
        return t.swapaxes(0, 1).reshape(B, S, H_A, HD)

    return from_blocks(o_c), from_blocks(o_s), from_blocks(o_w)


def dsa_prompt(q_b, kv_b, q_i, k_i, w_i):
    B, S = q_b.shape[:2]
    nqb = S // QBLK
    k_top = min(DSA_TOPK, S // 4)
    pos_k = jnp.arange(S, dtype=jnp.int32)
    bi = jnp.arange(B)[:, None, None]

    def block(args):
        i, qb, qi, wi = args
        pos_q = i * QBLK + jnp.arange(QBLK, dtype=jnp.int32)
        idx, valid = dsa_select(qi, wi, k_i, pos_q, pos_k, k_top)
        return dsa_attend(qb, kv_b[bi, idx], valid)

    def to_blocks(t):
        return t.reshape((B, nqb, QBLK) + t.shape[2:]).swapaxes(0, 1)

    o = lax.map(block, (jnp.arange(nqb, dtype=jnp.int32), to_blocks(q_b), to_blocks(q_i), to_blocks(w_i)))
    return o.swapaxes(0, 1).reshape(B, S, W_B)


def prompt_layer(x, mem, pos, norm_g, w_in_b, qk_g, w_cmp, pe_cmp, w_pool, pool_scale, mem_norm_g, w_mem_kv_b,
                 w_br_b, w_out_b):
    S = x.shape[1]
    f = front(x, pos, norm_g, w_in_b, qk_g)
    kc, vc = compress(f['kv_cmp'], w_cmp, pe_cmp, qk_g[1])
    o_c, o_s, o_w = nsa_prompt(f['q_n'], f['q_r'], kc, vc, f['kv_sel'], f['kv_win'])
    o_a = nsa_combine(f['g_a'], o_c, o_s, o_w)
    o_b = dsa_prompt(f['q_b'], f['kv_b'], f['q_i'], f['k_i'], f['w_i'])
    u_ext = jnp.pad(f['u_c'], ((0, 0), (POOL_BUF, 0), (0, 0)))
    o_p = pool_mix(u_ext, pos, w_pool, pool_scale)
    kv_m = mem_kv(mem, mem_norm_g, w_mem_kv_b, qk_g[7])
    o_m = mem_attend(f['q_m'], kv_m)
    y = merge(x, (o_a, o_b, o_p, o_m), f['silu'], f['gates'], w_br_b, w_out_b)
    wb = min(WINDOW, S)
    return y, (f['kv_cmp'], f['kv_sel'], f['kv_b'], f['k_i'], f['kv_win'][:, S - wb:], f['u_c'][:, S - POOL_BUF:],
               kv_m)


def sample_layer(x, pos, l, page_table, cache_cmp, cache_sel, cache_dsa, cache_idx, win_buf, pool_buf, mem_cache,
                 norm_g, w_in_b, qk_g, w_cmp, pe_cmp, w_pool, pool_scale, w_br_b, w_out_b):
    T = x.shape[1]
    past = page_table.shape[1] * PAGE_SIZE
    L = past + T
    f = front(x, pos, norm_g, w_in_b, qk_g)
    cmp_rows = jnp.concatenate([gather_pages(cache_cmp, l, page_table), f['kv_cmp']], axis=1)
    kc, vc = compress(cmp_rows, w_cmp, pe_cmp, qk_g[1])
    o_c, p_c = cmp_attend(f['q_n'], pos, kc, vc)
    idx, valid = select_blocks(p_c, pos, -(-L // SEL_LEN))
    kg, vg = gather_sel_sample(cache_sel, l, page_table, f['kv_sel'], idx, past)
    o_s = sel_attend(f['q_r'], pos, kg, vg, idx, valid)
    wb = win_buf.shape[1]
    win_rows = jnp.concatenate([win_buf, f['kv_win']], axis=1)
    pos_k = past - wb + jnp.arange(wb + T, dtype=jnp.int32)
    o_w = win_attend(f['q_r'], pos, win_rows[:, :, 0], win_rows[:, :, 1], pos_k)
    o_a = nsa_combine(f['g_a'], o_c, o_s, o_w)
    k_i_all = jnp.concatenate([gather_pages(cache_idx, l, page_table), f['k_i']], axis=1)
    idx_b, valid_b = dsa_select(f['q_i'], f['w_i'], k_i_all, pos, jnp.arange(L, dtype=jnp.int32),
                                min(DSA_TOPK, L // 4))
    kvg = gather_rows_sample(cache_dsa, l, page_table, f['kv_b'], idx_b, past)
    o_b = dsa_attend(f['q_b'], kvg, valid_b)
    u_ext = jnp.concatenate([pool_buf, f['u_c']], axis=1)
    o_p = pool_mix(u_ext, pos, w_pool, pool_scale)
    o_m = mem_attend(f['q_m'], mem_cache)
    y = merge(x, (o_a, o_b, o_p, o_m), f['silu'], f['gates'], w_br_b, w_out_b)
    return y, (f['kv_cmp'], f['kv_sel'], f['kv_b'], f['k_i'], win_rows[:, T:], u_ext[:, T:])


def kernel(x_prompt, x_sample, mem_prompt, cache_cmp, cache_sel, cache_dsa, cache_idx, state_win, state_pool,
           cache_mem, page_table, norm_g, w_in, qk_g, w_cmp, pe_cmp, w_pool, pool_scale, mem_norm_g, w_mem_kv,
           w_br, w_out):
    past = page_table.shape[1] * PAGE_SIZE
    pos_p = jnp.arange(x_prompt.shape[1], dtype=jnp.int32)
    pos_s = past + jnp.arange(x_sample.shape[1], dtype=jnp.int32)
    xp, xs = x_prompt, x_sample
    st_p, st_s = [], []
    for l in range(DEPTH):
        w_in_b = prep_w_in(w_in[l])
        w_br_b = w_br[l].astype(jnp.bfloat16)
        w_out_b = w_out[l].astype(jnp.bfloat16)
        w_mem_kv_b = w_mem_kv[l].astype(jnp.bfloat16)
        xp, sp = prompt_layer(xp, mem_prompt, pos_p, norm_g[l], w_in_b, qk_g[l], w_cmp[l], pe_cmp[l], w_pool[l],
                              pool_scale[l], mem_norm_g[l], w_mem_kv_b, w_br_b, w_out_b)
        xs, ss = sample_layer(xs, pos_s, l, page_table, cache_cmp, cache_sel, cache_dsa, cache_idx, state_win[l],
                              state_pool[l], cache_mem[l], norm_g[l], w_in_b, qk_g[l], w_cmp[l], pe_cmp[l],
                              w_pool[l], pool_scale[l], w_br_b, w_out_b)
        st_p.append(sp)
        st_s.append(ss)

    def stack(states, i):
        return jnp.stack([s[i] for s in states], axis=0)

    return (xp, xs,
            stack(st_p, 0), stack(st_s, 0),
            stack(st_p, 1), stack(st_s, 1),
            stack(st_p, 2), stack(st_s, 2),
            stack(st_p, 3), stack(st_s, 3),
            stack(st_p, 4), stack(st_s, 4),
            stack(st_p, 5), stack(st_s, 5),
            stack(st_p, 6))
```

```python
import functools

import numpy as np
import jax
import jax.numpy as jnp
from jax import lax
from jax.experimental import pallas as pl
from jax.experimental.pallas import tpu as pltpu

D_MODEL = 4096
DEPTH = 2
PAGE_SIZE = 128
HD = 128
H_A = 16
G_A = 2
R_A = H_A // G_A
CMP_LEN = 32
CMP_STRIDE = 16
SEL_LEN = 64
SEL_CHUNKS = SEL_LEN // CMP_STRIDE
N_SEL = 16
WINDOW = 512
H_B = 16
H_I = 32
D_I = 128
DSA_TOPK = 256
C_GROUPS = 4
POOL_WINDOWS = (2, 4, 8, 16)
C_WIDTH = 2048
C_GW = C_WIDTH // C_GROUPS
POOL_BUF = max(POOL_WINDOWS) - 1
H_M = 4
N_BRANCH = 4
W_A = H_A * HD
W_B = H_B * HD
W_M = H_M * HD
QBLK = 128
ROPE_THETA = 10000.0
EPS = 1e-6
SCALE = HD ** -0.5
NEG = -1e30
FORCE = 1e9

LANES = 128
SEG_NAMES = ('q_a', 'kv_c', 'kv_s', 'kv_w', 'g_a', 's_a', 'q_b', 'kv_b', 'q_i', 'k_i', 'w_i', 's_b',
             'u_c', 's_c', 'q_m', 's_m', 'g_m')
SEG_SIZES = (W_A, 2 * G_A * HD, 2 * G_A * HD, 2 * G_A * HD, 3 * H_A, W_A, W_B, 2 * HD, H_I * D_I, D_I, H_I, W_B,
             C_WIDTH, C_WIDTH, W_M, W_M, N_BRANCH * D_MODEL)
SEG_PAD = tuple(-(-s // LANES) * LANES for s in SEG_SIZES)
SEG_OFF = dict(zip(SEG_NAMES, np.concatenate([[0], np.cumsum(SEG_PAD)[:-1]]).tolist()))
SEG_LEN = dict(zip(SEG_NAMES, SEG_SIZES))
TN_FRONT = 512
N_FRONT = -(-sum(SEG_PAD) // TN_FRONT) * TN_FRONT
VMEM_LIMIT = 48 * 1024 * 1024


def _rmsnorm_cast_body(x_ref, g_ref, o_ref):
    x = x_ref[...]
    y = x * lax.rsqrt(jnp.mean(x * x, axis=-1, keepdims=True) + EPS)
    o_ref[...] = (y * g_ref[...]).astype(o_ref.dtype)


def rmsnorm_cast(x, g, tm):
    m, d = x.shape
    return pl.pallas_call(
        _rmsnorm_cast_body,
        grid=(m // tm,),
        in_specs=[pl.BlockSpec((tm, d), lambda i: (i, 0)), pl.BlockSpec((1, d), lambda i: (0, 0))],
        out_specs=pl.BlockSpec((tm, d), lambda i: (i, 0)),
        out_shape=jax.ShapeDtypeStruct((m, d), jnp.bfloat16),
        compiler_params=pltpu.CompilerParams(dimension_semantics=("parallel",), vmem_limit_bytes=VMEM_LIMIT),
        name="rmsnorm_cast",
    )(x, g.reshape(1, d))


def _matmul_body(a_ref, w_ref, o_ref):
    o_ref[...] = jnp.dot(a_ref[...], w_ref[...], preferred_element_type=jnp.float32)


def matmul(a, w, tm, tn):
    m, k = a.shape
    n = w.shape[1]
    return pl.pallas_call(
        _matmul_body,
        grid=(m // tm, n // tn),
        in_specs=[pl.BlockSpec((tm, k), lambda i, j: (i, 0)), pl.BlockSpec((k, tn), lambda i, j: (0, j))],
        out_specs=pl.BlockSpec((tm, tn), lambda i, j: (i, j)),
        out_shape=jax.ShapeDtypeStruct((m, n), jnp.float32),
        compiler_params=pltpu.CompilerParams(dimension_semantics=("parallel", "parallel"),
                                             vmem_limit_bytes=VMEM_LIMIT),
        name="matmul",
    )(a, w)


def _row_tile(m):
    return 1024 if m % 1024 == 0 else m


def dense(x2d, w_bf16):
    m = x2d.shape[0]
    n = w_bf16.shape[1]
    tn = 512 if n % 512 == 0 else n
    return matmul(x2d.astype(jnp.bfloat16), w_bf16, _row_tile(m), tn)


INT_MIN = -2 ** 31
N_CLASS = 4
CLASS_KEYS = 512
WIN_KEYS = WINDOW + QBLK


def _dot_nt(a, b):
    return lax.dot_general(a, b, (((1,), (1,)), ((), ())), preferred_element_type=jnp.float32)


def _attend(q, k, v, bias):
    s = _dot_nt(q, k) * SCALE + bias
    e = jnp.exp(s - jnp.max(s, axis=1, keepdims=True))
    d = jnp.sum(e, axis=1, keepdims=True)
    return jnp.dot(e.astype(jnp.bfloat16), v, preferred_element_type=jnp.float32) * (1.0 / d)


def _nsa_body(qn_ref, qr_ref, ga_ref, kc_ref, vc_ref, ks_ref, vs_ref, kw_ref, vw_ref, a_ref, e_ref, o_ref,
              bias_ref, wbias_ref, *, nk, qb0):
    q0 = pl.multiple_of((pl.program_id(1) + qb0) * QBLK, QBLK)
    pos = q0 + lax.broadcasted_iota(jnp.int32, (QBLK, 1), 0)
    lane = lax.broadcasted_iota(jnp.int32, (1, LANES), 1)
    gates = jax.nn.sigmoid(ga_ref[...])
    n_cmp = kc_ref.shape[1] - 1
    n_blk = e_ref.shape[1] // SEL_LEN

    wk = lax.broadcasted_iota(jnp.int32, (1, WIN_KEYS), 1) + (q0 - WINDOW)
    dist = pos - wk
    wbias_ref[...] = jnp.where(dist >= 0, jnp.where(dist < WINDOW, jnp.where(wk >= 0, 0.0, NEG), NEG), NEG)
    kidx = lax.broadcasted_iota(jnp.int32, (1, nk), 1)
    cmask = (lane < n_cmp) & (lane * CMP_STRIDE + (CMP_LEN - 1) <= pos)
    cur = pos // SEL_LEN
    forced = (lane == 0) | (lane == cur) | (lane == cur - 1)

    for g in range(G_A):
        pg = jnp.zeros((QBLK, LANES), jnp.float32)
        for r in range(R_A):
            h = g * R_A + r
            s = jnp.where(cmask, _dot_nt(qn_ref[h], kc_ref[g]) * SCALE, NEG)
            e = jnp.where(cmask, jnp.exp(s - jnp.max(s, axis=1, keepdims=True)), 0.0)
            p = e / jnp.maximum(jnp.sum(e, axis=1, keepdims=True), 1e-30)
            pg = pg + p
            o_c = jnp.dot(p.astype(jnp.bfloat16), vc_ref[g], preferred_element_type=jnp.float32)
            o_ref[:, h * HD:(h + 1) * HD] = gates[:, h:h + 1] * o_c

        hi = pg.astype(jnp.bfloat16)
        r1 = pg - hi.astype(jnp.float32)
        mid = r1.astype(jnp.bfloat16)
        lo = (r1 - mid.astype(jnp.float32)).astype(jnp.bfloat16)
        a = a_ref[...]
        score = (jnp.dot(hi, a, preferred_element_type=jnp.float32)
                 + jnp.dot(mid, a, preferred_element_type=jnp.float32)
                 + jnp.dot(lo, a, preferred_element_type=jnp.float32))
        score = jnp.where(forced, FORCE, jnp.where(lane <= cur, score, NEG))
        score = jnp.where(lane < n_blk, score, -3e38)
        rank = jnp.zeros((QBLK, LANES), jnp.int32)
        for jp in range(n_blk):
            col = score[:, jp:jp + 1]
            beats = jnp.where(col > score, 1, jnp.where(col == score, jnp.where(lane > jp, 1, 0), 0))
            rank = rank + beats
        sel = jnp.where(rank < min(N_SEL, n_blk), jnp.where(score > 0.5 * NEG, 1.0, 0.0), 0.0)
        selk = jnp.dot(sel.astype(jnp.bfloat16), e_ref[:, :nk], preferred_element_type=jnp.float32)
        bias_ref[...] = jnp.where(kidx <= pos, jnp.where(selk > 0.5, 0.0, NEG), NEG)

        kw = kw_ref[g, pl.ds(q0, WIN_KEYS), :]
        vw = vw_ref[g, pl.ds(q0, WIN_KEYS), :]
        for r in range(R_A):
            h = g * R_A + r
            q = qr_ref[h]
            o_s = _attend(q, ks_ref[g, :nk, :], vs_ref[g, :nk, :], bias_ref[...])
            o_w = _attend(q, kw, vw, wbias_ref[...])
            o_ref[:, h * HD:(h + 1) * HD] += (gates[:, H_A + h:H_A + h + 1] * o_s
                                             + gates[:, 2 * H_A + h:2 * H_A + h + 1] * o_w)


def _sel_matrices(n_cmp, n_blk):
    a = np.zeros((LANES, LANES), np.float32)
    for i in range(n_cmp):
        for c in (i, i + 1):
            if c // SEL_CHUNKS < n_blk:
                a[i, c // SEL_CHUNKS] += 1.0
    e = np.zeros((LANES, n_blk * SEL_LEN), np.float32)
    for j in range(n_blk):
        e[j, j * SEL_LEN:(j + 1) * SEL_LEN] = 1.0
    return jnp.asarray(a, jnp.bfloat16), jnp.asarray(e, jnp.bfloat16)


def nsa_prompt_attention(qn, qr, ga, kc, vc, ks, vs, kw, vw):
    B, nqb = qn.shape[:2]
    S = nqb * QBLK
    a_mat, e_mat = _sel_matrices(S // CMP_STRIDE - 1, S // SEL_LEN)
    per = nqb // N_CLASS
    outs = []
    for c in range(N_CLASS):
        nk = (c + 1) * per * QBLK
        qmap = lambda b, j, c=c: (b, c * per + j, 0, 0, 0)
        bmap = lambda b, j: (b, 0, 0, 0)
        outs.append(pl.pallas_call(
            functools.partial(_nsa_body, nk=nk, qb0=c * per),
            grid=(B, per),
            in_specs=[pl.BlockSpec((None, None, H_A, QBLK, HD), qmap),
                      pl.BlockSpec((None, None, H_A, QBLK, HD), qmap),
                      pl.BlockSpec((None, QBLK, LANES), lambda b, j, c=c: (b, c * per + j, 0)),
                      pl.BlockSpec((None, G_A, LANES, HD), bmap),
                      pl.BlockSpec((None, G_A, LANES, HD), bmap),
                      pl.BlockSpec((None, G_A, S, HD), bmap),
                      pl.BlockSpec((None, G_A, S, HD), bmap),
                      pl.BlockSpec((None, G_A, WINDOW + S, HD), bmap),
                      pl.BlockSpec((None, G_A, WINDOW + S, HD), bmap),
                      pl.BlockSpec((LANES, LANES), lambda b, j: (0, 0)),
                      pl.BlockSpec((LANES, S), lambda b, j: (0, 0))],
            out_specs=pl.BlockSpec((None, QBLK, W_A), lambda b, j: (b, j, 0)),
            out_shape=jax.ShapeDtypeStruct((B, per * QBLK, W_A), jnp.float32),
            scratch_shapes=[pltpu.VMEM((QBLK, nk), jnp.float32), pltpu.VMEM((QBLK, WIN_KEYS), jnp.float32)],
            compiler_params=pltpu.CompilerParams(dimension_semantics=("parallel", "arbitrary"),
                                                 vmem_limit_bytes=VMEM_LIMIT),
            name=f"nsa_prompt_{nk}",
        )(qn, qr, ga, kc, vc, ks, vs, kw, vw, a_mat, e_mat))
    return jnp.concatenate(outs, axis=1)


IDX_CHUNK = 512


def _dsa_body(qi_ref, wi_ref, ki_ref, qb_ref, kb_ref, vb_ref, o_ref, score_ref, key_ref, bias_ref, *, nk, qb0, k_top):
    q0 = (pl.program_id(1) + qb0) * QBLK
    pos = q0 + lax.broadcasted_iota(jnp.int32, (QBLK, 1), 0)
    kidx = lax.broadcasted_iota(jnp.int32, (1, nk), 1)

    for c in range(nk // IDX_CHUNK):
        k = ki_ref[c * IDX_CHUNK:(c + 1) * IDX_CHUNK, :]
        acc = jnp.zeros((QBLK, IDX_CHUNK), jnp.float32)
        for h in range(H_I):
            acc = acc + wi_ref[:, h:h + 1] * jnp.maximum(_dot_nt(qi_ref[h], k), 0.0)
        score_ref[:, c * IDX_CHUNK:(c + 1) * IDX_CHUNK] = acc * ((D_I ** -0.5) * (H_I ** -0.5))

    sc = jnp.where(kidx <= pos, score_ref[...], NEG)
    score_ref[...] = sc
    bits = pltpu.bitcast(sc, jnp.int32)
    key_ref[...] = jnp.where(bits >= 0, bits, bits ^ 0x7FFFFFFF)

    def step(it, ans):
        cand = ans | jnp.left_shift(jnp.int32(1), 31 - it)
        cnt = jnp.sum(jnp.where(key_ref[...] >= (cand ^ INT_MIN), 1, 0), axis=1, keepdims=True)
        return jnp.where(cnt >= k_top, cand, ans)

    thr = lax.fori_loop(0, 32, step, jnp.zeros((QBLK, 1), jnp.int32)) ^ INT_MIN
    bias_ref[...] = jnp.where(key_ref[...] >= thr, jnp.where(score_ref[...] > 0.5 * NEG, 0.0, NEG), NEG)

    for h in range(H_B):
        o_ref[:, h * HD:(h + 1) * HD] = _attend(qb_ref[h], kb_ref[:nk, :], vb_ref[:nk, :], bias_ref[...])


def dsa_prompt_attention(qi, wi, ki, qb, kb, vb):
    B, nqb = qi.shape[:2]
    S = nqb * QBLK
    k_top = min(DSA_TOPK, S // 4)
    per = nqb // N_CLASS
    outs = []
    for c in range(N_CLASS):
        nk = (c + 1) * per * QBLK
        qmap = lambda b, j, c=c: (b, c * per + j, 0, 0, 0)
        kmap = lambda b, j: (b, 0, 0)
        outs.append(pl.pallas_call(
            functools.partial(_dsa_body, nk=nk, qb0=c * per, k_top=k_top),
            grid=(B, per),
            in_specs=[pl.BlockSpec((None, None, H_I, QBLK, D_I), qmap),
                      pl.BlockSpec((None, QBLK, H_I), lambda b, j, c=c: (b, c * per + j, 0)),
                      pl.BlockSpec((None, S, D_I), kmap),
                      pl.BlockSpec((None, None, H_B, QBLK, HD), qmap),
                      pl.BlockSpec((None, S, HD), kmap),
                      pl.BlockSpec((None, S, HD), kmap)],
            out_specs=pl.BlockSpec((None, QBLK, W_B), lambda b, j: (b, j, 0)),
            out_shape=jax.ShapeDtypeStruct((B, per * QBLK, W_B), jnp.float32),
            scratch_shapes=[pltpu.VMEM((QBLK, nk), jnp.float32), pltpu.VMEM((QBLK, nk), jnp.int32),
                            pltpu.VMEM((QBLK, nk), jnp.float32)],
            compiler_params=pltpu.CompilerParams(dimension_semantics=("parallel", "arbitrary"),
                                                 vmem_limit_bytes=VMEM_LIMIT),
            name=f"dsa_prompt_{nk}",
        )(qi, wi, ki, qb, kb, vb))
    return jnp.concatenate(outs, axis=1)


def prep_w_in(w):
    parts, off = [], 0
    for size, pad in zip(SEG_SIZES, SEG_PAD):
        p = w[:, off:off + size].astype(jnp.bfloat16)
        if pad > size:
            p = jnp.pad(p, ((0, 0), (0, pad - size)))
        parts.append(p)
        off += size
    tail = N_FRONT - sum(SEG_PAD)
    if tail:
        parts.append(jnp.zeros((w.shape[0], tail), jnp.bfloat16))
    return jnp.concatenate(parts, axis=1)


def seg(z, name):
    return z[..., SEG_OFF[name]:SEG_OFF[name] + SEG_LEN[name]]


def rmsnorm(x, g):
    y = x * lax.rsqrt(jnp.mean(x * x, axis=-1, keepdims=True) + EPS)
    return y * g


def rope(x, pos):
    half = x.shape[-1] // 2
    inv_freq = ROPE_THETA ** (-jnp.arange(half, dtype=jnp.float32) / half)
    ang = pos.astype(jnp.float32)[:, None] * inv_freq[None, :]
    cos = jnp.cos(ang)[:, None, :]
    sin = jnp.sin(ang)[:, None, :]
    x1, x2 = x[..., :half], x[..., half:]
    return jnp.concatenate([x1 * cos - x2 * sin, x2 * cos + x1 * sin], axis=-1)


def masked_softmax(s, mask):
    s = jnp.where(mask, s, NEG)
    return jnp.where(mask, jax.nn.softmax(s, axis=-1), 0.0)


def front(x, pos, norm_g, w_in_b, qk_g):
    B, T, _ = x.shape
    x2 = x.reshape(B * T, D_MODEL)
    h = rmsnorm_cast(x2, norm_g, 512 if (B * T) % 512 == 0 else B * T)
    z = matmul(h, w_in_b, _row_tile(B * T), TN_FRONT).reshape(B, T, N_FRONT)
    q_n = rmsnorm(seg(z, 'q_a').reshape(B, T, H_A, HD), qk_g[0])
    kv_s = seg(z, 'kv_s').reshape(B, T, 2, G_A, HD)
    kv_w = seg(z, 'kv_w').reshape(B, T, 2, G_A, HD)
    kv_b = seg(z, 'kv_b').reshape(B, T, 2, HD)
    k_b = rope(rmsnorm(kv_b[:, :, 0], qk_g[5])[:, :, None], pos)[:, :, 0]
    return {
        'q_n': q_n,
        'q_r': rope(q_n, pos),
        'kv_cmp': seg(z, 'kv_c').reshape(B, T, 2, G_A, HD),
        'kv_sel': jnp.stack([rope(rmsnorm(kv_s[:, :, 0], qk_g[2]), pos), kv_s[:, :, 1]], axis=2),
        'kv_win': jnp.stack([rope(rmsnorm(kv_w[:, :, 0], qk_g[3]), pos), kv_w[:, :, 1]], axis=2),
        'g_a': seg(z, 'g_a').reshape(B, T, 3, H_A),
        'q_b': rope(rmsnorm(seg(z, 'q_b').reshape(B, T, H_B, HD), qk_g[4]), pos),
        'kv_b': jnp.stack([k_b, kv_b[:, :, 1]], axis=2),
        'q_i': rope(seg(z, 'q_i').reshape(B, T, H_I, D_I), pos),
        'k_i': rope(seg(z, 'k_i')[:, :, None], pos)[:, :, 0],
        'w_i': seg(z, 'w_i'),
        'u_c': seg(z, 'u_c'),
        'q_m': rmsnorm(seg(z, 'q_m').reshape(B, T, H_M, HD), qk_g[6]),
        'silu': (seg(z, 's_a'), seg(z, 's_b'), seg(z, 's_c'), seg(z, 's_m')),
        'gates': seg(z, 'g_m'),
        'g_a_pad': z[..., SEG_OFF['g_a']:SEG_OFF['g_a'] + LANES],
    }


def compress(rows, w_cmp, pe_cmp, g_k):
    B, L = rows.shape[:2]
    n_ch = L // CMP_STRIDE
    ch = rows[:, :n_ch * CMP_STRIDE].reshape(B, n_ch, CMP_STRIDE, 2, G_A, HD)
    ch = jnp.moveaxis(ch, 3, 0)
    pe = pe_cmp[:, None, None]
    c = (jnp.einsum('kbncgd,kgcde->kbnge', ch[:, :, :-1] + pe[:, :, :, :CMP_STRIDE], w_cmp[:, :, :CMP_STRIDE])
         + jnp.einsum('kbncgd,kgcde->kbnge', ch[:, :, 1:] + pe[:, :, :, CMP_STRIDE:], w_cmp[:, :, CMP_STRIDE:]))
    return rmsnorm(c[0], g_k), c[1]


def cmp_attend(q_n, pos_q, kc, vc):
    B, T = q_n.shape[:2]
    qg = q_n.reshape(B, T, G_A, R_A, HD)
    s = jnp.einsum('btgrd,bngd->btgrn', qg, kc) * SCALE
    blk_end = jnp.arange(kc.shape[1], dtype=jnp.int32) * CMP_STRIDE + (CMP_LEN - 1)
    mask = blk_end[None, :] <= pos_q[:, None]
    p = masked_softmax(s, mask[None, :, None, None, :])
    o = jnp.einsum('btgrn,bngd->btgrd', p, vc)
    return o.reshape(B, T, H_A, HD), p


def select_blocks(p_cmp, pos_q, n_blk):
    pg = p_cmp.sum(axis=3)
    zero = jnp.zeros(pg.shape[:-1] + (1,), pg.dtype)
    chunk = jnp.concatenate([pg, zero], -1) + jnp.concatenate([zero, pg], -1)
    chunk = jnp.pad(chunk, ((0, 0), (0, 0), (0, 0), (0, n_blk * SEL_CHUNKS - chunk.shape[-1])))
    score = chunk.reshape(chunk.shape[:-1] + (n_blk, SEL_CHUNKS)).sum(-1)
    j = jnp.arange(n_blk, dtype=jnp.int32)[None, :]
    cur = (pos_q // SEL_LEN)[:, None]
    forced = (j == 0) | (j == cur) | (j == cur - 1)
    valid = j <= cur
    score = jnp.where(forced[None, :, None], FORCE, jnp.where(valid[None, :, None], score, NEG))
    top, idx = lax.top_k(score, min(N_SEL, n_blk))
    return idx, top > 0.5 * NEG


def sel_attend(q_r, pos_q, kb, vb, idx, valid):
    B, T = q_r.shape[:2]
    n = idx.shape[-1] * SEL_LEN
    qg = q_r.reshape(B, T, G_A, R_A, HD)
    s = jnp.einsum('btgrd,btgkud->btgrku', qg, kb).reshape(B, T, G_A, R_A, n) * SCALE
    kpos = idx[..., None] * SEL_LEN + jnp.arange(SEL_LEN, dtype=jnp.int32)
    mask = (kpos <= pos_q[None, :, None, None, None]) & valid[..., None]
    p = masked_softmax(s, mask.reshape(B, T, G_A, 1, n))
    o = jnp.einsum('btgrn,btgnd->btgrd', p, vb.reshape(B, T, G_A, n, HD))
    return o.reshape(B, T, H_A, HD)


def win_attend(q_r, pos_q, kw, vw, pos_k):
    B, T = q_r.shape[:2]
    qg = q_r.reshape(B, T, G_A, R_A, HD)
    s = jnp.einsum('btgrd,bsgd->btgrs', qg, kw) * SCALE
    d = pos_q[:, None] - pos_k[None, :]
    mask = (d >= 0) & (d < WINDOW) & (pos_k[None, :] >= 0)
    p = masked_softmax(s, mask[None, :, None, None, :])
    o = jnp.einsum('btgrs,bsgd->btgrd', p, vw)
    return o.reshape(B, T, H_A, HD)


def nsa_combine(g_a, o_c, o_s, o_w):
    g = jax.nn.sigmoid(g_a)[..., None]
    o = g[:, :, 0] * o_c + g[:, :, 1] * o_s + g[:, :, 2] * o_w
    return o.reshape(o.shape[0], o.shape[1], W_A)


def dsa_select(q_i, w_i, k_i, pos_q, pos_k, k_top):
    logits = jnp.einsum('bthd,bsd->bths', q_i, k_i) * (D_I ** -0.5)
    score = jnp.einsum('bths,bth->bts', jax.nn.relu(logits), w_i) * (H_I ** -0.5)
    score = jnp.where((pos_k[None, :] <= pos_q[:, None])[None], score, NEG)
    top, idx = lax.top_k(score, k_top)
    return idx, top > 0.5 * NEG


def dsa_attend(q_b, kvg, valid):
    B, T = q_b.shape[:2]
    s = jnp.einsum('bthd,btkd->bthk', q_b, kvg[..., 0, :]) * SCALE
    p = masked_softmax(s, valid[:, :, None, :])
    o = jnp.einsum('bthk,btkd->bthd', p, kvg[..., 1, :])
    return o.reshape(B, T, W_B)


def pool_mix(u_ext, pos_q, w_pool, pool_scale):
    B, n_ext, _ = u_ext.shape
    T = n_ext - POOL_BUF
    cs = jnp.concatenate([jnp.zeros((B, 1, C_WIDTH), jnp.float32), jnp.cumsum(u_ext, axis=1)], axis=1)
    end = cs[:, POOL_BUF + 1:]
    means = []
    for gi, w in enumerate(POOL_WINDOWS):
        ch = slice(gi * C_GW, (gi + 1) * C_GW)
        start = cs[:, POOL_BUF + 1 - w: POOL_BUF + 1 - w + T, ch]
        cnt = jnp.minimum(pos_q + 1, w).astype(jnp.float32)[None, :, None]
        means.append((end[:, :, ch] - start) / cnt)
    d = jnp.concatenate(means, axis=-1) - u_ext[:, POOL_BUF:]
    y = jnp.einsum('btgc,gce->btge', d.reshape(B, T, C_GROUPS, C_GW), w_pool)
    return y.reshape(B, T, C_WIDTH) * pool_scale


def mem_kv(mem, mem_norm_g, w_mem_kv_b, g_k):
    B, M, _ = mem.shape
    h = rmsnorm_cast(mem.reshape(B * M, D_MODEL), mem_norm_g, 512)
    kv = matmul(h, w_mem_kv_b, _row_tile(B * M), 512).reshape(B, M, 2, H_M, HD)
    return jnp.stack([rmsnorm(kv[:, :, 0], g_k), kv[:, :, 1]], axis=2)


def mem_attend(q_m, kv):
    B, T = q_m.shape[:2]
    s = jnp.einsum('bthd,bmhd->bthm', q_m, kv[:, :, 0]) * SCALE
    p = jax.nn.softmax(s, axis=-1)
    o = jnp.einsum('bthm,bmhd->bthd', p, kv[:, :, 1])
    return o.reshape(B, T, W_M)


def merge(x, outs, silus, gates, w_br_b, w_out_b):
    B, T, _ = x.shape
    g = jax.nn.sigmoid(gates).reshape(B, T, N_BRANCH, D_MODEL)
    y = jnp.zeros_like(x)
    row = 0
    for b in range(N_BRANCH):
        width = outs[b].shape[-1]
        a = (outs[b] * jax.nn.silu(silus[b])).reshape(B * T, width)
        y = y + g[:, :, b] * dense(a, w_br_b[row:row + width]).reshape(B, T, D_MODEL)
        row += width
    return x + dense(y.reshape(B * T, D_MODEL), w_out_b).reshape(B, T, D_MODEL)


def gather_pages(pool, l, page_table):
    rows = pool[l, page_table]
    return rows.reshape((page_table.shape[0], -1) + rows.shape[3:])


def gather_sel_sample(pool, l, page_table, new_rows, idx, past):
    Bd, T = new_rows.shape[:2]
    n_past_blk = past // SEL_LEN
    n_new_blk = -(-T // SEL_LEN)
    blk_per_page = PAGE_SIZE // SEL_LEN
    bi = jnp.arange(Bd)[:, None, None, None]
    gi = jnp.arange(G_A)[None, None, :, None]
    u = jnp.arange(SEL_LEN, dtype=jnp.int32)
    pb = jnp.minimum(idx, n_past_blk - 1)
    page = page_table[bi, pb // blk_per_page][..., None]
    off = ((pb % blk_per_page) * SEL_LEN)[..., None] + u
    past_rows = pool[l, page, off, :, gi[..., None], :]
    new_pad = jnp.pad(new_rows, ((0, 0), (0, n_new_blk * SEL_LEN - T), (0, 0), (0, 0), (0, 0)))
    new_pad = new_pad.reshape(Bd, n_new_blk, SEL_LEN, 2, G_A, HD)
    nb = jnp.clip(idx - n_past_blk, 0, n_new_blk - 1)
    new_g = new_pad[bi, nb, :, :, gi, :]
    res = jnp.where((idx < n_past_blk)[..., None, None, None], past_rows, new_g)
    return res[..., 0, :], res[..., 1, :]


def gather_rows_sample(pool, l, page_table, new_rows, idx, past):
    Bd, T = new_rows.shape[:2]
    bi = jnp.arange(Bd)[:, None, None]
    pi = jnp.minimum(idx, past - 1)
    past_rows = pool[l, page_table[bi, pi // PAGE_SIZE], pi % PAGE_SIZE]
    new_g = new_rows[bi, jnp.clip(idx - past, 0, T - 1)]
    return jnp.where((idx < past)[..., None, None], past_rows, new_g)


def nsa_prompt(q_n, q_r, kc, vc, kv_sel, kv_win):
    B, S = q_n.shape[:2]
    nqb = S // QBLK
    ns = S // SEL_LEN
    sel_blocks = kv_sel.reshape(B, ns, SEL_LEN, 2, G_A, HD).transpose(3, 0, 4, 1, 2, 5)
    win_pad = jnp.pad(kv_win, ((0, 0), (WINDOW, 0), (0, 0), (0, 0), (0, 0)))
    bi = jnp.arange(B)[:, None, None, None]
    gi = jnp.arange(G_A)[None, None, :, None]

    def block(args):
        i, qn, qr = args
        start = i * QBLK
        pos_q = start + jnp.arange(QBLK, dtype=jnp.int32)
        o_c, p_c = cmp_attend(qn, pos_q, kc, vc)
        idx, valid = select_blocks(p_c, pos_q, ns)
        o_s = sel_attend(qr, pos_q, sel_blocks[0][bi, gi, idx], sel_blocks[1][bi, gi, idx], idx, valid)
        win = lax.dynamic_slice_in_dim(win_pad, start, QBLK + WINDOW, axis=1)
        pos_k = start - WINDOW + jnp.arange(QBLK + WINDOW, dtype=jnp.int32)
        o_w = win_attend(qr, pos_q, win[:, :, 0], win[:, :, 1], pos_k)
        return o_c, o_s, o_w

    def to_blocks(t):
        return t.reshape(B, nqb, QBLK, H_A, HD).swapaxes(0, 1)

    o_c, o_s, o_w = lax.map(block, (jnp.arange(nqb, dtype=jnp.int32), to_blocks(q_n), to_blocks(q_r)))

    def from_blocks(t):
        return t.swapaxes(0, 1).reshape(B, S, H_A, HD)

    return from_blocks(o_c), from_blocks(o_s), from_blocks(o_w)


def dsa_prompt(q_b, kv_b, q_i, k_i, w_i):
    B, S = q_b.shape[:2]
    nqb = S // QBLK
    k_top = min(DSA_TOPK, S // 4)
    pos_k = jnp.arange(S, dtype=jnp.int32)
    bi = jnp.arange(B)[:, None, None]

    def block(args):
        i, qb, qi, wi = args
        pos_q = i * QBLK + jnp.arange(QBLK, dtype=jnp.int32)
        idx, valid = dsa_select(qi, wi, k_i, pos_q, pos_k, k_top)
        return dsa_attend(qb, kv_b[bi, idx], valid)

    def to_blocks(t):
        return t.reshape((B, nqb, QBLK) + t.shape[2:]).swapaxes(0, 1)

    o = lax.map(block, (jnp.arange(nqb, dtype=jnp.int32), to_blocks(q_b), to_blocks(q_i), to_blocks(w_i)))
    return o.swapaxes(0, 1).reshape(B, S, W_B)


def head_blocks(t):
    B, S, H, d = t.shape
    return t.reshape(B, S // QBLK, QBLK, H, d).transpose(0, 1, 3, 2, 4).astype(jnp.bfloat16)


def prompt_layer(x, mem, pos, norm_g, w_in_b, qk_g, w_cmp, pe_cmp, w_pool, pool_scale, mem_norm_g, w_mem_kv_b,
                 w_br_b, w_out_b):
    S = x.shape[1]
    f = front(x, pos, norm_g, w_in_b, qk_g)
    kc, vc = compress(f['kv_cmp'], w_cmp, pe_cmp, qk_g[1])
    bf = jnp.bfloat16
    pad_c = lambda t: jnp.pad(t, ((0, 0), (0, LANES - t.shape[1]), (0, 0), (0, 0))).transpose(0, 2, 1, 3).astype(bf)
    by_group = lambda t: t.transpose(0, 2, 1, 3).astype(bf)
    kv_wp = jnp.pad(f['kv_win'], ((0, 0), (WINDOW, 0), (0, 0), (0, 0), (0, 0)))
    o_a = nsa_prompt_attention(head_blocks(f['q_n']), head_blocks(f['q_r']), f['g_a_pad'], pad_c(kc), pad_c(vc),
                               by_group(f['kv_sel'][:, :, 0]), by_group(f['kv_sel'][:, :, 1]),
                               by_group(kv_wp[:, :, 0]), by_group(kv_wp[:, :, 1]))
    o_b = dsa_prompt_attention(head_blocks(f['q_i']), f['w_i'], f['k_i'].astype(bf), head_blocks(f['q_b']),
                               f['kv_b'][:, :, 0].astype(bf), f['kv_b'][:, :, 1].astype(bf))
    u_ext = jnp.pad(f['u_c'], ((0, 0), (POOL_BUF, 0), (0, 0)))
    o_p = pool_mix(u_ext, pos, w_pool, pool_scale)
    kv_m = mem_kv(mem, mem_norm_g, w_mem_kv_b, qk_g[7])
    o_m = mem_attend(f['q_m'], kv_m)
    y = merge(x, (o_a, o_b, o_p, o_m), f['silu'], f['gates'], w_br_b, w_out_b)
    wb = min(WINDOW, S)
    return y, (f['kv_cmp'], f['kv_sel'], f['kv_b'], f['k_i'], f['kv_win'][:, S - wb:], f['u_c'][:, S - POOL_BUF:],
               kv_m)


def sample_layer(x, pos, l, page_table, cache_cmp, cache_sel, cache_dsa, cache_idx, win_buf, pool_buf, mem_cache,
                 norm_g, w_in_b, qk_g, w_cmp, pe_cmp, w_pool, pool_scale, w_br_b, w_out_b):
    T = x.shape[1]
    past = page_table.shape[1] * PAGE_SIZE
    L = past + T
    f = front(x, pos, norm_g, w_in_b, qk_g)
    cmp_rows = jnp.concatenate([gather_pages(cache_cmp, l, page_table), f['kv_cmp']], axis=1)
    kc, vc = compress(cmp_rows, w_cmp, pe_cmp, qk_g[1])
    o_c, p_c = cmp_attend(f['q_n'], pos, kc, vc)
    idx, valid = select_blocks(p_c, pos, -(-L // SEL_LEN))
    kg, vg = gather_sel_sample(cache_sel, l, page_table, f['kv_sel'], idx, past)
    o_s = sel_attend(f['q_r'], pos, kg, vg, idx, valid)
    wb = win_buf.shape[1]
    win_rows = jnp.concatenate([win_buf, f['kv_win']], axis=1)
    pos_k = past - wb + jnp.arange(wb + T, dtype=jnp.int32)
    o_w = win_attend(f['q_r'], pos, win_rows[:, :, 0], win_rows[:, :, 1], pos_k)
    o_a = nsa_combine(f['g_a'], o_c, o_s, o_w)
    k_i_all = jnp.concatenate([gather_pages(cache_idx, l, page_table), f['k_i']], axis=1)
    idx_b, valid_b = dsa_select(f['q_i'], f['w_i'], k_i_all, pos, jnp.arange(L, dtype=jnp.int32),
                                min(DSA_TOPK, L // 4))
    kvg = gather_rows_sample(cache_dsa, l, page_table, f['kv_b'], idx_b, past)
    o_b = dsa_attend(f['q_b'], kvg, valid_b)
    u_ext = jnp.concatenate([pool_buf, f['u_c']], axis=1)
    o_p = pool_mix(u_ext, pos, w_pool, pool_scale)
    o_m = mem_attend(f['q_m'], mem_cache)
    y = merge(x, (o_a, o_b, o_p, o_m), f['silu'], f['gates'], w_br_b, w_out_b)
    return y, (f['kv_cmp'], f['kv_sel'], f['kv_b'], f['k_i'], win_rows[:, T:], u_ext[:, T:])


def kernel(x_prompt, x_sample, mem_prompt, cache_cmp, cache_sel, cache_dsa, cache_idx, state_win, state_pool,
           cache_mem, page_table, norm_g, w_in, qk_g, w_cmp, pe_cmp, w_pool, pool_scale, mem_norm_g, w_mem_kv,
           w_br, w_out):
    past = page_table.shape[1] * PAGE_SIZE
    pos_p = jnp.arange(x_prompt.shape[1], dtype=jnp.int32)
    pos_s = past + jnp.arange(x_sample.shape[1], dtype=jnp.int32)
    xp, xs = x_prompt, x_sample
    st_p, st_s = [], []
    for l in range(DEPTH):
        w_in_b = prep_w_in(w_in[l])
        w_br_b = w_br[l].astype(jnp.bfloat16)
        w_out_b = w_out[l].astype(jnp.bfloat16)
        w_mem_kv_b = w_mem_kv[l].astype(jnp.bfloat16)
        xp, sp = prompt_layer(xp, mem_prompt, pos_p, norm_g[l], w_in_b, qk_g[l], w_cmp[l], pe_cmp[l], w_pool[l],
                              pool_scale[l], mem_norm_g[l], w_mem_kv_b, w_br_b, w_out_b)
        xs, ss = sample_layer(xs, pos_s, l, page_table, cache_cmp, cache_sel, cache_dsa, cache_idx, state_win[l],
                              state_pool[l], cache_mem[l], norm_g[l], w_in_b, qk_g[l], w_cmp[l], pe_cmp[l],
                              w_pool[l], pool_scale[l], w_br_b, w_out_b)
        st_p.append(sp)
        st_s.append(ss)

    def stack(states, i):
        return jnp.stack([s[i] for s in states], axis=0)

    return (xp, xs,
            stack(st_p, 0), stack(st_s, 0),
            stack(st_p, 1), stack(st_s, 1),
            stack(st_p, 2), stack(st_s, 2),
            stack(st_p, 3), stack(st_s, 3),
            stack(st_p, 4), stack(st_s, 4),
            stack(st_p, 5), stack(st_s, 5),
            stack(st_p, 6))
```

```python
import functools

import numpy as np
import jax
import jax.numpy as jnp
from jax import lax
from jax.experimental import pallas as pl
from jax.experimental.pallas import tpu as pltpu

D_MODEL = 4096
DEPTH = 2
PAGE_SIZE = 128
HD = 128
H_A = 16
G_A = 2
R_A = H_A // G_A
CMP_LEN = 32
CMP_STRIDE = 16
SEL_LEN = 64
SEL_CHUNKS = SEL_LEN // CMP_STRIDE
N_SEL = 16
WINDOW = 512
H_B = 16
H_I = 32
D_I = 128
DSA_TOPK = 256
C_GROUPS = 4
POOL_WINDOWS = (2, 4, 8, 16)
C_WIDTH = 2048
C_GW = C_WIDTH // C_GROUPS
POOL_BUF = max(POOL_WINDOWS) - 1
H_M = 4
N_BRANCH = 4
W_A = H_A * HD
W_B = H_B * HD
W_M = H_M * HD
W_BR = W_A + W_B + C_WIDTH + W_M
QBLK = 128
ROPE_THETA = 10000.0
EPS = 1e-6
SCALE = HD ** -0.5
NEG = -1e30
FORCE = 1e9
INT_MIN = -2 ** 31

LANES = 128
BF16 = jnp.bfloat16

SRC_NAMES = ('q_a', 'kv_c', 'kv_s', 'kv_w', 'g_a', 's_a', 'q_b', 'kv_b', 'q_i', 'k_i', 'w_i', 's_b',
             'u_c', 's_c', 'q_m', 's_m', 'g_m')
SRC_SIZES = (W_A, 2 * G_A * HD, 2 * G_A * HD, 2 * G_A * HD, 3 * H_A, W_A, W_B, 2 * HD, H_I * D_I, D_I, H_I, W_B,
             C_WIDTH, C_WIDTH, W_M, W_M, N_BRANCH * D_MODEL)
SRC_OFF = dict(zip(SRC_NAMES, np.concatenate([[0], np.cumsum(SRC_SIZES)[:-1]]).tolist()))
SEG_LEN = dict(zip(SRC_NAMES, SRC_SIZES))
DST_ORDER = ('q_a', 's_a', 'q_b', 's_b', 'u_c', 's_c', 'q_i', 'g_m', 'kv_c', 'kv_s', 'kv_w', 'q_m', 's_m', 'kv_b',
             'k_i', 'g_a', 'w_i')
SEG_PAD = {n: -(-SEG_LEN[n] // LANES) * LANES for n in DST_ORDER}
SEG_OFF = dict(zip(DST_ORDER, np.concatenate([[0], np.cumsum([SEG_PAD[n] for n in DST_ORDER])[:-1]]).tolist()))
TN_FRONT = 512
N_FRONT = -(-sum(SEG_PAD.values()) // TN_FRONT) * TN_FRONT
VMEM_LIMIT = 48 * 1024 * 1024

N_CLASS = 4
WIN_KEYS = WINDOW + QBLK
IDX_CHUNK = 512
TB_POST = 256


def _cparams(*sem):
    return pltpu.CompilerParams(dimension_semantics=sem, vmem_limit_bytes=VMEM_LIMIT)


def _blk(name, width):
    assert SEG_OFF[name] % width == 0
    return SEG_OFF[name] // width


def _rmsnorm_cast_body(x_ref, g_ref, o_ref):
    x = x_ref[...]
    y = x * lax.rsqrt(jnp.mean(x * x, axis=-1, keepdims=True) + EPS)
    o_ref[...] = (y * g_ref[...]).astype(o_ref.dtype)


def rmsnorm_cast(x, g, tm):
    m, d = x.shape
    return pl.pallas_call(
        _rmsnorm_cast_body,
        grid=(m // tm,),
        in_specs=[pl.BlockSpec((tm, d), lambda i: (i, 0)), pl.BlockSpec((1, d), lambda i: (0, 0))],
        out_specs=pl.BlockSpec((tm, d), lambda i: (i, 0)),
        out_shape=jax.ShapeDtypeStruct((m, d), BF16),
        compiler_params=_cparams("parallel"),
        name="rmsnorm_cast",
    )(x, g.reshape(1, d))


def _matmul_body(a_ref, w_ref, o_ref):
    o_ref[...] = jnp.dot(a_ref[...], w_ref[...], preferred_element_type=jnp.float32)


def matmul(a, w, tm, tn):
    m, k = a.shape
    n = w.shape[1]
    return pl.pallas_call(
        _matmul_body,
        grid=(m // tm, n // tn),
        in_specs=[pl.BlockSpec((tm, k), lambda i, j: (i, 0)), pl.BlockSpec((k, tn), lambda i, j: (0, j))],
        out_specs=pl.BlockSpec((tm, tn), lambda i, j: (i, j)),
        out_shape=jax.ShapeDtypeStruct((m, n), jnp.float32),
        compiler_params=_cparams("parallel", "parallel"),
        name="matmul",
    )(a, w)


def _row_tile(m, big=1024):
    return big if m % big == 0 else m


def _merge_body(aa, ab, ap, am, ga, gb, gp, gm, wa, wb, wp, wm, o_ref):
    acc = jax.nn.sigmoid(ga[...]) * jnp.dot(aa[...], wa[...], preferred_element_type=jnp.float32)
    acc += jax.nn.sigmoid(gb[...]) * jnp.dot(ab[...], wb[...], preferred_element_type=jnp.float32)
    acc += jax.nn.sigmoid(gp[...]) * jnp.dot(ap[...], wp[...], preferred_element_type=jnp.float32)
    acc += jax.nn.sigmoid(gm[...]) * jnp.dot(am[...], wm[...], preferred_element_type=jnp.float32)
    o_ref[...] = acc.astype(o_ref.dtype)


def merge_branches(acts, z, w_br_b):
    m = z.shape[0]
    tm, tn = _row_tile(m, 512), 512
    gate0 = _blk('g_m', tn)
    per = D_MODEL // tn
    widths = (W_A, W_B, C_WIDTH, W_M)
    rows = np.concatenate([[0], np.cumsum(widths)[:-1]]).tolist()
    act_specs = [pl.BlockSpec((tm, w), lambda i, j: (i, 0)) for w in widths]
    gate_specs = [pl.BlockSpec((tm, tn), lambda i, j, b=b: (i, gate0 + b * per + j)) for b in range(N_BRANCH)]
    w_specs = [pl.BlockSpec((w, tn), lambda i, j, r=r, w=w: (r // w, j)) for w, r in zip(widths, rows)]
    assert all(r % w == 0 for w, r in zip(widths, rows))
    return pl.pallas_call(
        _merge_body,
        grid=(m // tm, D_MODEL // tn),
        in_specs=act_specs + gate_specs + w_specs,
        out_specs=pl.BlockSpec((tm, tn), lambda i, j: (i, j)),
        out_shape=jax.ShapeDtypeStruct((m, D_MODEL), BF16),
        compiler_params=_cparams("parallel", "arbitrary"),
        name="merge_branches",
    )(*acts, z, z, z, z, w_br_b, w_br_b, w_br_b, w_br_b)


def _out_body(x_ref, y_ref, w_ref, o_ref):
    o_ref[...] = x_ref[...] + jnp.dot(y_ref[...], w_ref[...], preferred_element_type=jnp.float32)


def out_proj(x2, y_mid, w_out_b):
    m = x2.shape[0]
    tm, tn = _row_tile(m), 512
    return pl.pallas_call(
        _out_body,
        grid=(m // tm, D_MODEL // tn),
        in_specs=[pl.BlockSpec((tm, tn), lambda i, j: (i, j)), pl.BlockSpec((tm, D_MODEL), lambda i, j: (i, 0)),
                  pl.BlockSpec((D_MODEL, tn), lambda i, j: (0, j))],
        out_specs=pl.BlockSpec((tm, tn), lambda i, j: (i, j)),
        out_shape=jax.ShapeDtypeStruct((m, D_MODEL), jnp.float32),
        compiler_params=_cparams("parallel", "arbitrary"),
        name="out_proj",
    )(x2, y_mid, w_out_b)


def _dot_nt(a, b):
    return lax.dot_general(a, b, (((1,), (1,)), ((), ())), preferred_element_type=jnp.float32)


def _rms(x, g):
    return x * lax.rsqrt(jnp.mean(x * x, axis=-1, keepdims=True) + EPS) * g


def _rope(y, cos, sin):
    return y * cos + pltpu.roll(y, HD // 2, 1) * sin


def _silu(s):
    return s * jax.nn.sigmoid(s)


def _attend(q, k, v, bias):
    s = _dot_nt(q, k) * SCALE + bias
    e = jnp.exp(s - jnp.max(s, axis=1, keepdims=True))
    d = jnp.sum(e, axis=1, keepdims=True)
    return jnp.dot(e.astype(BF16), v, preferred_element_type=jnp.float32) * (1.0 / d)


def _kpost_body(zs_ref, zw_ref, zb_ref, zi_ref, cos_ref, sin_ref, g_ref,
                sel_o, win_o, dsa_o, idx_o, ks_o, vs_o, kw_o, vw_o, kb_o, vb_o, ki_o):
    cos = cos_ref[...]
    sin = sin_ref[...]
    for z_ref, st_o, k_o, v_o, gi in ((zs_ref, sel_o, ks_o, vs_o, 2), (zw_ref, win_o, kw_o, vw_o, 3)):
        for g in range(G_A):
            k = _rope(_rms(z_ref[:, g * HD:(g + 1) * HD], g_ref[gi:gi + 1, :]), cos, sin)
            v = z_ref[:, (G_A + g) * HD:(G_A + g + 1) * HD]
            st_o[:, g * HD:(g + 1) * HD] = k
            st_o[:, (G_A + g) * HD:(G_A + g + 1) * HD] = v
            k_o[g] = k.astype(BF16)
            v_o[g] = v.astype(BF16)
    k = _rope(_rms(zb_ref[:, :HD], g_ref[5:6, :]), cos, sin)
    v = zb_ref[:, HD:]
    dsa_o[:, :HD] = k
    dsa_o[:, HD:] = v
    kb_o[...] = k.astype(BF16)
    vb_o[...] = v.astype(BF16)
    ki = _rope(zi_ref[...], cos, sin)
    idx_o[...] = ki
    ki_o[...] = ki.astype(BF16)


def key_post(z, cos, sin, qk_g, B, S):
    tb = TB_POST
    nb = S // tb
    zmap = lambda name, w: pl.BlockSpec((tb, w), lambda b, j: (b * nb + j, _blk(name, w)))
    row = lambda w: pl.BlockSpec((None, tb, w), lambda b, j: (b, j, 0))
    grp = pl.BlockSpec((None, G_A, tb, HD), lambda b, j: (b, 0, j, 0))
    f32, sds = jnp.float32, jax.ShapeDtypeStruct
    return pl.pallas_call(
        _kpost_body,
        grid=(B, nb),
        in_specs=[zmap('kv_s', 512), zmap('kv_w', 512), zmap('kv_b', 256), zmap('k_i', 128),
                  pl.BlockSpec((tb, HD), lambda b, j: (j, 0)), pl.BlockSpec((tb, HD), lambda b, j: (j, 0)),
                  pl.BlockSpec((8, HD), lambda b, j: (0, 0))],
        out_specs=[row(512), row(512), row(256), row(128), grp, grp, grp, grp, row(HD), row(HD), row(D_I)],
        out_shape=[sds((B, S, 512), f32), sds((B, S, 512), f32), sds((B, S, 256), f32), sds((B, S, D_I), f32),
                   sds((B, G_A, S, HD), BF16), sds((B, G_A, S, HD), BF16), sds((B, G_A, S, HD), BF16),
                   sds((B, G_A, S, HD), BF16), sds((B, S, HD), BF16), sds((B, S, HD), BF16), sds((B, S, D_I), BF16)],
        compiler_params=_cparams("parallel", "parallel"),
        name="key_post",
    )(z, z, z, z, cos, sin, qk_g)


def _nsa_body(zq_ref, zs_ref, ga_ref, cos_ref, sin_ref, g_ref, kc_ref, vc_ref, ks_ref, vs_ref, kw_ref, vw_ref,
              a_ref, e_ref, o_ref, qr_s, acc_s, sel_s, bias_s, wbias_s, *, per):
    j = pl.program_id(1)
    q0 = pl.multiple_of(j * QBLK, QBLK)
    pos = q0 + lax.broadcasted_iota(jnp.int32, (QBLK, 1), 0)
    lane = lax.broadcasted_iota(jnp.int32, (1, LANES), 1)
    gates = jax.nn.sigmoid(ga_ref[...])
    cos = cos_ref[...]
    sin = sin_ref[...]
    n_cmp = kc_ref.shape[1] - 1
    n_blk = e_ref.shape[1] // SEL_LEN

    w0 = pl.multiple_of(jnp.maximum(q0 - WINDOW, 0), QBLK)
    dist = pos - (w0 + lax.broadcasted_iota(jnp.int32, (1, WIN_KEYS), 1))
    wbias_s[...] = jnp.where(dist >= 0, jnp.where(dist < WINDOW, 0.0, NEG), NEG)
    cmask = (lane < n_cmp) & (lane * CMP_STRIDE + (CMP_LEN - 1) <= pos)
    cur = pos // SEL_LEN
    forced = (lane == 0) | (lane == cur) | (lane == cur - 1)

    for g in range(G_A):
        kw = kw_ref[g, pl.ds(w0, WIN_KEYS), :]
        vw = vw_ref[g, pl.ds(w0, WIN_KEYS), :]
        pg = jnp.zeros((QBLK, LANES), jnp.float32)
        for r in range(R_A):
            h = g * R_A + r
            qn = _rms(zq_ref[:, h * HD:(h + 1) * HD], g_ref[0:1, :])
            qr = _rope(qn, cos, sin).astype(BF16)
            qr_s[h] = qr
            s = jnp.where(cmask, _dot_nt(qn.astype(BF16), kc_ref[g]) * SCALE, NEG)
            e = jnp.where(cmask, jnp.exp(s - jnp.max(s, axis=1, keepdims=True)), 0.0)
            p = e / jnp.maximum(jnp.sum(e, axis=1, keepdims=True), 1e-30)
            pg = pg + p
            o_c = jnp.dot(p.astype(BF16), vc_ref[g], preferred_element_type=jnp.float32)
            o_w = _attend(qr, kw, vw, wbias_s[...])
            acc_s[:, h * HD:(h + 1) * HD] = (gates[:, h:h + 1] * o_c
                                            + gates[:, 2 * H_A + h:2 * H_A + h + 1] * o_w)

        hi = pg.astype(BF16)
        r1 = pg - hi.astype(jnp.float32)
        mid = r1.astype(BF16)
        lo = (r1 - mid.astype(jnp.float32)).astype(BF16)
        a = a_ref[...]
        score = (jnp.dot(hi, a, preferred_element_type=jnp.float32)
                 + jnp.dot(mid, a, preferred_element_type=jnp.float32)
                 + jnp.dot(lo, a, preferred_element_type=jnp.float32))
        score = jnp.where(forced, FORCE, jnp.where(lane <= cur, score, NEG))
        score = jnp.where(lane < n_blk, score, -3e38)
        rank = jnp.zeros((QBLK, LANES), jnp.int32)
        for jp in range(n_blk):
            col = score[:, jp:jp + 1]
            rank = rank + jnp.where(col > score, 1, jnp.where(col == score, jnp.where(lane > jp, 1, 0), 0))
        sel_s[g] = jnp.where(rank < min(N_SEL, n_blk), jnp.where(score > 0.5 * NEG, 1.0, 0.0), 0.0)

    def selected(nk):
        kidx = lax.broadcasted_iota(jnp.int32, (1, nk), 1)
        for g in range(G_A):
            selk = jnp.dot(sel_s[g].astype(BF16), e_ref[:, :nk], preferred_element_type=jnp.float32)
            bias_s[:, :nk] = jnp.where(kidx <= pos, jnp.where(selk > 0.5, 0.0, NEG), NEG)
            for r in range(R_A):
                h = g * R_A + r
                o_s = _attend(qr_s[h], ks_ref[g, :nk, :], vs_ref[g, :nk, :], bias_s[:, :nk])
                o = acc_s[:, h * HD:(h + 1) * HD] + gates[:, H_A + h:H_A + h + 1] * o_s
                o_ref[:, h * HD:(h + 1) * HD] = (o * _silu(zs_ref[:, h * HD:(h + 1) * HD])).astype(o_ref.dtype)

    for c in range(N_CLASS):
        pl.when(j // per == c)(functools.partial(selected, (c + 1) * per * QBLK))


def _sel_matrices(n_cmp, n_blk):
    a = np.zeros((LANES, LANES), np.float32)
    for i in range(n_cmp):
        for c in (i, i + 1):
            if c // SEL_CHUNKS < n_blk:
                a[i, c // SEL_CHUNKS] += 1.0
    e = np.zeros((LANES, n_blk * SEL_LEN), np.float32)
    for j in range(n_blk):
        e[j, j * SEL_LEN:(j + 1) * SEL_LEN] = 1.0
    return jnp.asarray(a, BF16), jnp.asarray(e, BF16)


def nsa_prompt_attention(z, cos, sin, qk_g, kc, vc, ks, vs, kw, vw, B, S):
    nqb = S // QBLK
    assert nqb % N_CLASS == 0 and S >= WIN_KEYS and S // CMP_STRIDE - 1 < LANES
    a_mat, e_mat = _sel_matrices(S // CMP_STRIDE - 1, S // SEL_LEN)
    zmap = lambda name, w: pl.BlockSpec((QBLK, w), lambda b, j: (b * nqb + j, _blk(name, w)))
    tab = pl.BlockSpec((QBLK, HD), lambda b, j: (j, 0))
    per_b = lambda n: pl.BlockSpec((None, G_A, n, HD), lambda b, j: (b, 0, 0, 0))
    return pl.pallas_call(
        functools.partial(_nsa_body, per=nqb // N_CLASS),
        grid=(B, nqb),
        in_specs=[zmap('q_a', W_A), zmap('s_a', W_A), zmap('g_a', LANES), tab, tab,
                  pl.BlockSpec((8, HD), lambda b, j: (0, 0)),
                  per_b(LANES), per_b(LANES), per_b(S), per_b(S), per_b(S), per_b(S),
                  pl.BlockSpec((LANES, LANES), lambda b, j: (0, 0)), pl.BlockSpec((LANES, S), lambda b, j: (0, 0))],
        out_specs=pl.BlockSpec((QBLK, W_A), lambda b, j: (b * nqb + j, 0)),
        out_shape=jax.ShapeDtypeStruct((B * S, W_A), BF16),
        scratch_shapes=[pltpu.VMEM((H_A, QBLK, HD), BF16), pltpu.VMEM((QBLK, W_A), jnp.float32),
                        pltpu.VMEM((G_A, QBLK, LANES), jnp.float32), pltpu.VMEM((QBLK, S), jnp.float32),
                        pltpu.VMEM((QBLK, WIN_KEYS), jnp.float32)],
        compiler_params=_cparams("parallel", "arbitrary"),
        name="nsa_prompt",
    )(z, z, z, cos, sin, qk_g, kc, vc, ks, vs, kw, vw, a_mat, e_mat)


def _dsa_body(zqi_ref, wi_ref, zqb_ref, zs_ref, cos_ref, sin_ref, g_ref, ki_ref, kb_ref, vb_ref, o_ref,
              qi_s, qb_s, score_s, key_s, bias_s, *, per, k_top):
    j = pl.program_id(1)
    pos = j * QBLK + lax.broadcasted_iota(jnp.int32, (QBLK, 1), 0)
    cos = cos_ref[...]
    sin = sin_ref[...]
    for h in range(H_I):
        qi_s[h] = _rope(zqi_ref[:, h * D_I:(h + 1) * D_I], cos, sin).astype(BF16)
    for h in range(H_B):
        qb_s[h] = _rope(_rms(zqb_ref[:, h * HD:(h + 1) * HD], g_ref[4:5, :]), cos, sin).astype(BF16)

    def attend(nk):
        kidx = lax.broadcasted_iota(jnp.int32, (1, nk), 1)
        for c in range(nk // IDX_CHUNK):
            k = ki_ref[c * IDX_CHUNK:(c + 1) * IDX_CHUNK, :]
            acc = jnp.zeros((QBLK, IDX_CHUNK), jnp.float32)
            for h in range(H_I):
                acc = acc + wi_ref[:, h:h + 1] * jnp.maximum(_dot_nt(qi_s[h], k), 0.0)
            score_s[:, c * IDX_CHUNK:(c + 1) * IDX_CHUNK] = acc * ((D_I ** -0.5) * (H_I ** -0.5))
        sc = jnp.where(kidx <= pos, score_s[:, :nk], NEG)
        score_s[:, :nk] = sc
        bits = pltpu.bitcast(sc, jnp.int32)
        key_s[:, :nk] = jnp.where(bits >= 0, bits, bits ^ 0x7FFFFFFF)

        def value_step(it, ans):
            cand = ans | jnp.left_shift(jnp.int32(1), 31 - it)
            cnt = jnp.sum(jnp.where(key_s[:, :nk] >= (cand ^ INT_MIN), 1, 0), axis=1, keepdims=True)
            return jnp.where(cnt >= k_top, cand, ans)

        thr = lax.fori_loop(0, 32, value_step, jnp.zeros((QBLK, 1), jnp.int32)) ^ INT_MIN
        need = k_top - jnp.sum(jnp.where(key_s[:, :nk] > thr, 1, 0), axis=1, keepdims=True)
        idx_bits = max(1, (nk - 1).bit_length())

        def index_step(it, ans):
            cand = ans | jnp.left_shift(jnp.int32(1), idx_bits - 1 - it)
            below = jnp.where(key_s[:, :nk] == thr, jnp.where(kidx < cand, 1, 0), 0)
            return jnp.where(jnp.sum(below, axis=1, keepdims=True) < need, cand, ans)

        cut = lax.fori_loop(0, idx_bits, index_step, jnp.zeros((QBLK, 1), jnp.int32))
        key = key_s[:, :nk]
        chosen = jnp.where(key > thr, 1, jnp.where(key == thr, jnp.where(kidx <= cut, 1, 0), 0))
        bias_s[:, :nk] = jnp.where(chosen > 0, jnp.where(score_s[:, :nk] > 0.5 * NEG, 0.0, NEG), NEG)

        for h in range(H_B):
            o = _attend(qb_s[h], kb_ref[:nk, :], vb_ref[:nk, :], bias_s[:, :nk])
            o_ref[:, h * HD:(h + 1) * HD] = (o * _silu(zs_ref[:, h * HD:(h + 1) * HD])).astype(o_ref.dtype)

    for c in range(N_CLASS):
        pl.when(j // per == c)(functools.partial(attend, (c + 1) * per * QBLK))


def dsa_prompt_attention(z, cos, sin, qk_g, ki, kb, vb, B, S):
    nqb = S // QBLK
    per = nqb // N_CLASS
    assert nqb % N_CLASS == 0 and (per * QBLK) % IDX_CHUNK == 0
    zmap = lambda name, w: pl.BlockSpec((QBLK, w), lambda b, j: (b * nqb + j, _blk(name, w)))
    tab = pl.BlockSpec((QBLK, HD), lambda b, j: (j, 0))
    per_b = lambda d: pl.BlockSpec((None, S, d), lambda b, j: (b, 0, 0))
    return pl.pallas_call(
        functools.partial(_dsa_body, per=per, k_top=min(DSA_TOPK, S // 4)),
        grid=(B, nqb),
        in_specs=[zmap('q_i', H_I * D_I), zmap('w_i', LANES), zmap('q_b', W_B), zmap('s_b', W_B), tab, tab,
                  pl.BlockSpec((8, HD), lambda b, j: (0, 0)), per_b(D_I), per_b(HD), per_b(HD)],
        out_specs=pl.BlockSpec((QBLK, W_B), lambda b, j: (b * nqb + j, 0)),
        out_shape=jax.ShapeDtypeStruct((B * S, W_B), BF16),
        scratch_shapes=[pltpu.VMEM((H_I, QBLK, D_I), BF16), pltpu.VMEM((H_B, QBLK, HD), BF16),
                        pltpu.VMEM((QBLK, S), jnp.float32), pltpu.VMEM((QBLK, S), jnp.int32),
                        pltpu.VMEM((QBLK, S), jnp.float32)],
        compiler_params=_cparams("parallel", "arbitrary"),
        name="dsa_prompt",
    )(z, z, z, z, cos, sin, qk_g, ki, kb, vb)


def _pool_body(u_ref, up_ref, s_ref, w_ref, sc_ref, o_ref, ext_s):
    j = pl.program_id(1)
    tb = u_ref.shape[0]
    halo = POOL_BUF + 1
    prev = up_ref[tb - halo:, :]
    ext_s[:halo, :] = jnp.where(j > 0, prev, 0.0)
    ext_s[halo:, :] = u_ref[...]
    pos = j * tb + lax.broadcasted_iota(jnp.int32, (tb, 1), 0)
    for gi, w in enumerate(POOL_WINDOWS):
        cols = slice(gi * C_GW, (gi + 1) * C_GW)
        u = u_ref[:, cols]
        tot = u
        for k in range(1, w):
            tot = tot + ext_s[halo - k:halo - k + tb, cols]
        cnt = jnp.minimum(pos + 1, w).astype(jnp.float32)
        d = tot / cnt - u
        y = jnp.dot(d.astype(BF16), w_ref[gi], preferred_element_type=jnp.float32) * sc_ref[:, cols]
        o_ref[:, cols] = (y * _silu(s_ref[:, cols])).astype(o_ref.dtype)


def pool_prompt(z, w_pool_b, pool_scale, B, S):
    tb = TB_POST
    nb = S // tb
    ublk = _blk('u_c', C_WIDTH)
    return pl.pallas_call(
        _pool_body,
        grid=(B, nb),
        in_specs=[pl.BlockSpec((tb, C_WIDTH), lambda b, j: (b * nb + j, ublk)),
                  pl.BlockSpec((tb, C_WIDTH), lambda b, j: (b * nb + jnp.maximum(j - 1, 0), ublk)),
                  pl.BlockSpec((tb, C_WIDTH), lambda b, j: (b * nb + j, _blk('s_c', C_WIDTH))),
                  pl.BlockSpec((C_GROUPS, C_GW, C_GW), lambda b, j: (0, 0, 0)),
                  pl.BlockSpec((1, C_WIDTH), lambda b, j: (0, 0))],
        out_specs=pl.BlockSpec((tb, C_WIDTH), lambda b, j: (b * nb + j, 0)),
        out_shape=jax.ShapeDtypeStruct((B * S, C_WIDTH), BF16),
        scratch_shapes=[pltpu.VMEM((tb + POOL_BUF + 1, C_WIDTH), jnp.float32)],
        compiler_params=_cparams("parallel", "arbitrary"),
        name="pool_prompt",
    )(z, z, z, w_pool_b, pool_scale.reshape(1, C_WIDTH))


def _mem_body(zq_ref, zs_ref, g_ref, k_ref, v_ref, o_ref):
    zero = jnp.zeros((1, k_ref.shape[1]), jnp.float32)
    for h in range(H_M):
        cols = slice(h * HD, (h + 1) * HD)
        q = _rms(zq_ref[:, cols], g_ref[6:7, :]).astype(BF16)
        o = _attend(q, k_ref[h], v_ref[h], zero)
        o_ref[:, cols] = (o * _silu(zs_ref[:, cols])).astype(o_ref.dtype)


def mem_prompt_attention(z, qk_g, km, vm, B, S):
    tb = 2 * TB_POST
    nb = S // tb
    n_mem = km.shape[2]
    return pl.pallas_call(
        _mem_body,
        grid=(B, nb),
        in_specs=[pl.BlockSpec((tb, W_M), lambda b, j: (b * nb + j, _blk('q_m', W_M))),
                  pl.BlockSpec((tb, W_M), lambda b, j: (b * nb + j, _blk('s_m', W_M))),
                  pl.BlockSpec((8, HD), lambda b, j: (0, 0)),
                  pl.BlockSpec((None, H_M, n_mem, HD), lambda b, j: (b, 0, 0, 0)),
                  pl.BlockSpec((None, H_M, n_mem, HD), lambda b, j: (b, 0, 0, 0))],
        out_specs=pl.BlockSpec((tb, W_M), lambda b, j: (b * nb + j, 0)),
        out_shape=jax.ShapeDtypeStruct((B * S, W_M), BF16),
        compiler_params=_cparams("parallel", "parallel"),
        name="mem_prompt",
    )(z, z, qk_g, km, vm)


def prep_w_in(w):
    parts = []
    for name in DST_ORDER:
        p = w[:, SRC_OFF[name]:SRC_OFF[name] + SEG_LEN[name]].astype(BF16)
        if SEG_PAD[name] > SEG_LEN[name]:
            p = jnp.pad(p, ((0, 0), (0, SEG_PAD[name] - SEG_LEN[name])))
        parts.append(p)
    tail = N_FRONT - sum(SEG_PAD.values())
    if tail:
        parts.append(jnp.zeros((w.shape[0], tail), BF16))
    return jnp.concatenate(parts, axis=1)


def seg(z, name):
    return z[..., SEG_OFF[name]:SEG_OFF[name] + SEG_LEN[name]]


def rope_tables(pos):
    half = HD // 2
    inv_freq = ROPE_THETA ** (-jnp.arange(half, dtype=jnp.float32) / half)
    ang = pos.astype(jnp.float32)[:, None] * inv_freq[None, :]
    c, s = jnp.cos(ang), jnp.sin(ang)
    return jnp.concatenate([c, c], axis=-1), jnp.concatenate([-s, s], axis=-1)


def project(x2, norm_g, w_in_b):
    m = x2.shape[0]
    h = rmsnorm_cast(x2, norm_g, _row_tile(m, 512))
    return matmul(h, w_in_b, _row_tile(m), TN_FRONT)


def rmsnorm(x, g):
    y = x * lax.rsqrt(jnp.mean(x * x, axis=-1, keepdims=True) + EPS)
    return y * g


def rope(x, pos):
    half = x.shape[-1] // 2
    inv_freq = ROPE_THETA ** (-jnp.arange(half, dtype=jnp.float32) / half)
    ang = pos.astype(jnp.float32)[:, None] * inv_freq[None, :]
    cos = jnp.cos(ang)[:, None, :]
    sin = jnp.sin(ang)[:, None, :]
    x1, x2 = x[..., :half], x[..., half:]
    return jnp.concatenate([x1 * cos - x2 * sin, x2 * cos + x1 * sin], axis=-1)


def masked_softmax(s, mask):
    s = jnp.where(mask, s, NEG)
    return jnp.where(mask, jax.nn.softmax(s, axis=-1), 0.0)


def front_sample(z, pos, qk_g):
    B, T, _ = z.shape
    q_n = rmsnorm(seg(z, 'q_a').reshape(B, T, H_A, HD), qk_g[0])
    kv_s = seg(z, 'kv_s').reshape(B, T, 2, G_A, HD)
    kv_w = seg(z, 'kv_w').reshape(B, T, 2, G_A, HD)
    kv_b = seg(z, 'kv_b').reshape(B, T, 2, HD)
    k_b = rope(rmsnorm(kv_b[:, :, 0], qk_g[5])[:, :, None], pos)[:, :, 0]
    return {
        'q_n': q_n,
        'q_r': rope(q_n, pos),
        'kv_cmp': seg(z, 'kv_c').reshape(B, T, 2, G_A, HD),
        'kv_sel': jnp.stack([rope(rmsnorm(kv_s[:, :, 0], qk_g[2]), pos), kv_s[:, :, 1]], axis=2),
        'kv_win': jnp.stack([rope(rmsnorm(kv_w[:, :, 0], qk_g[3]), pos), kv_w[:, :, 1]], axis=2),
        'g_a': seg(z, 'g_a').reshape(B, T, 3, H_A),
        'q_b': rope(rmsnorm(seg(z, 'q_b').reshape(B, T, H_B, HD), qk_g[4]), pos),
        'kv_b': jnp.stack([k_b, kv_b[:, :, 1]], axis=2),
        'q_i': rope(seg(z, 'q_i').reshape(B, T, H_I, D_I), pos),
        'k_i': rope(seg(z, 'k_i')[:, :, None], pos)[:, :, 0],
        'w_i': seg(z, 'w_i'),
        'u_c': seg(z, 'u_c'),
        'q_m': rmsnorm(seg(z, 'q_m').reshape(B, T, H_M, HD), qk_g[6]),
        'silu': (seg(z, 's_a'), seg(z, 's_b'), seg(z, 's_c'), seg(z, 's_m')),
    }


def compress(rows, w_cmp, pe_cmp, g_k):
    B, L = rows.shape[:2]
    n_ch = L // CMP_STRIDE
    ch = rows[:, :n_ch * CMP_STRIDE].reshape(B, n_ch, CMP_STRIDE, 2, G_A, HD)
    ch = jnp.moveaxis(ch, 3, 0)
    pe = pe_cmp[:, None, None]
    c = (jnp.einsum('kbncgd,kgcde->kbnge', ch[:, :, :-1] + pe[:, :, :, :CMP_STRIDE], w_cmp[:, :, :CMP_STRIDE])
         + jnp.einsum('kbncgd,kgcde->kbnge', ch[:, :, 1:] + pe[:, :, :, CMP_STRIDE:], w_cmp[:, :, CMP_STRIDE:]))
    return rmsnorm(c[0], g_k), c[1]


def cmp_attend(q_n, pos_q, kc, vc):
    B, T = q_n.shape[:2]
    qg = q_n.reshape(B, T, G_A, R_A, HD)
    s = jnp.einsum('btgrd,bngd->btgrn', qg, kc) * SCALE
    blk_end = jnp.arange(kc.shape[1], dtype=jnp.int32) * CMP_STRIDE + (CMP_LEN - 1)
    mask = blk_end[None, :] <= pos_q[:, None]
    p = masked_softmax(s, mask[None, :, None, None, :])
    o = jnp.einsum('btgrn,bngd->btgrd', p, vc)
    return o.reshape(B, T, H_A, HD), p


def select_blocks(p_cmp, pos_q, n_blk):
    pg = p_cmp.sum(axis=3)
    zero = jnp.zeros(pg.shape[:-1] + (1,), pg.dtype)
    chunk = jnp.concatenate([pg, zero], -1) + jnp.concatenate([zero, pg], -1)
    chunk = jnp.pad(chunk, ((0, 0), (0, 0), (0, 0), (0, n_blk * SEL_CHUNKS - chunk.shape[-1])))
    score = chunk.reshape(chunk.shape[:-1] + (n_blk, SEL_CHUNKS)).sum(-1)
    j = jnp.arange(n_blk, dtype=jnp.int32)[None, :]
    cur = (pos_q // SEL_LEN)[:, None]
    forced = (j == 0) | (j == cur) | (j == cur - 1)
    valid = j <= cur
    score = jnp.where(forced[None, :, None], FORCE, jnp.where(valid[None, :, None], score, NEG))
    top, idx = lax.top_k(score, min(N_SEL, n_blk))
    return idx, top > 0.5 * NEG


def sel_attend(q_r, pos_q, kb, vb, idx, valid):
    B, T = q_r.shape[:2]
    n = idx.shape[-1] * SEL_LEN
    qg = q_r.reshape(B, T, G_A, R_A, HD)
    s = jnp.einsum('btgrd,btgkud->btgrku', qg, kb).reshape(B, T, G_A, R_A, n) * SCALE
    kpos = idx[..., None] * SEL_LEN + jnp.arange(SEL_LEN, dtype=jnp.int32)
    mask = (kpos <= pos_q[None, :, None, None, None]) & valid[..., None]
    p = masked_softmax(s, mask.reshape(B, T, G_A, 1, n))
    o = jnp.einsum('btgrn,btgnd->btgrd', p, vb.reshape(B, T, G_A, n, HD))
    return o.reshape(B, T, H_A, HD)


def win_attend(q_r, pos_q, kw, vw, pos_k):
    B, T = q_r.shape[:2]
    qg = q_r.reshape(B, T, G_A, R_A, HD)
    s = jnp.einsum('btgrd,bsgd->btgrs', qg, kw) * SCALE
    d = pos_q[:, None] - pos_k[None, :]
    mask = (d >= 0) & (d < WINDOW) & (pos_k[None, :] >= 0)
    p = masked_softmax(s, mask[None, :, None, None, :])
    o = jnp.einsum('btgrs,bsgd->btgrd', p, vw)
    return o.reshape(B, T, H_A, HD)


def nsa_combine(g_a, o_c, o_s, o_w):
    g = jax.nn.sigmoid(g_a)[..., None]
    o = g[:, :, 0] * o_c + g[:, :, 1] * o_s + g[:, :, 2] * o_w
    return o.reshape(o.shape[0], o.shape[1], W_A)


def dsa_select(q_i, w_i, k_i, pos_q, pos_k, k_top):
    logits = jnp.einsum('bthd,bsd->bths', q_i, k_i) * (D_I ** -0.5)
    score = jnp.einsum('bths,bth->bts', jax.nn.relu(logits), w_i) * (H_I ** -0.5)
    score = jnp.where((pos_k[None, :] <= pos_q[:, None])[None], score, NEG)
    top, idx = lax.top_k(score, k_top)
    return idx, top > 0.5 * NEG


def dsa_attend(q_b, kvg, valid):
    B, T = q_b.shape[:2]
    s = jnp.einsum('bthd,btkd->bthk', q_b, kvg[..., 0, :]) * SCALE
    p = masked_softmax(s, valid[:, :, None, :])
    o = jnp.einsum('bthk,btkd->bthd', p, kvg[..., 1, :])
    return o.reshape(B, T, W_B)


def pool_mix(u_ext, pos_q, w_pool, pool_scale):
    B, n_ext, _ = u_ext.shape
    T = n_ext - POOL_BUF
    cs = jnp.concatenate([jnp.zeros((B, 1, C_WIDTH), jnp.float32), jnp.cumsum(u_ext, axis=1)], axis=1)
    end = cs[:, POOL_BUF + 1:]
    means = []
    for gi, w in enumerate(POOL_WINDOWS):
        ch = slice(gi * C_GW, (gi + 1) * C_GW)
        start = cs[:, POOL_BUF + 1 - w: POOL_BUF + 1 - w + T, ch]
        cnt = jnp.minimum(pos_q + 1, w).astype(jnp.float32)[None, :, None]
        means.append((end[:, :, ch] - start) / cnt)
    d = jnp.concatenate(means, axis=-1) - u_ext[:, POOL_BUF:]
    y = jnp.einsum('btgc,gce->btge', d.reshape(B, T, C_GROUPS, C_GW), w_pool)
    return y.reshape(B, T, C_WIDTH) * pool_scale


def mem_kv(mem, mem_norm_g, w_mem_kv_b, g_k):
    B, M, _ = mem.shape
    h = rmsnorm_cast(mem.reshape(B * M, D_MODEL), mem_norm_g, _row_tile(B * M, 512))
    kv = matmul(h, w_mem_kv_b, _row_tile(B * M), 512).reshape(B, M, 2, H_M, HD)
    return jnp.stack([rmsnorm(kv[:, :, 0], g_k), kv[:, :, 1]], axis=2)


def mem_attend(q_m, kv):
    B, T = q_m.shape[:2]
    s = jnp.einsum('bthd,bmhd->bthm', q_m, kv[:, :, 0]) * SCALE
    p = jax.nn.softmax(s, axis=-1)
    o = jnp.einsum('bthm,bmhd->bthd', p, kv[:, :, 1])
    return o.reshape(B, T, W_M)


def gather_pages(pool, l, page_table):
    rows = pool[l, page_table]
    return rows.reshape((page_table.shape[0], -1) + rows.shape[3:])


def gather_sel_sample(pool, l, page_table, new_rows, idx, past):
    Bd, T = new_rows.shape[:2]
    n_past_blk = past // SEL_LEN
    n_new_blk = -(-T // SEL_LEN)
    blk_per_page = PAGE_SIZE // SEL_LEN
    bi = jnp.arange(Bd)[:, None, None, None]
    gi = jnp.arange(G_A)[None, None, :, None]
    u = jnp.arange(SEL_LEN, dtype=jnp.int32)
    pb = jnp.minimum(idx, n_past_blk - 1)
    page = page_table[bi, pb // blk_per_page][..., None]
    off = ((pb % blk_per_page) * SEL_LEN)[..., None] + u
    past_rows = pool[l, page, off, :, gi[..., None], :]
    new_pad = jnp.pad(new_rows, ((0, 0), (0, n_new_blk * SEL_LEN - T), (0, 0), (0, 0), (0, 0)))
    new_pad = new_pad.reshape(Bd, n_new_blk, SEL_LEN, 2, G_A, HD)
    nb = jnp.clip(idx - n_past_blk, 0, n_new_blk - 1)
    new_g = new_pad[bi, nb, :, :, gi, :]
    res = jnp.where((idx < n_past_blk)[..., None, None, None], past_rows, new_g)
    return res[..., 0, :], res[..., 1, :]


def gather_rows_sample(pool, l, page_table, new_rows, idx, past):
    Bd, T = new_rows.shape[:2]
    bi = jnp.arange(Bd)[:, None, None]
    pi = jnp.minimum(idx, past - 1)
    past_rows = pool[l, page_table[bi, pi // PAGE_SIZE], pi % PAGE_SIZE]
    new_g = new_rows[bi, jnp.clip(idx - past, 0, T - 1)]
    return jnp.where((idx < past)[..., None, None], past_rows, new_g)


def prompt_layer(x, mem, pos, norm_g, w_in_b, qk_g, w_cmp, pe_cmp, w_pool_b, pool_scale, mem_norm_g, w_mem_kv_b,
                 w_br_b, w_out_b):
    B, S, _ = x.shape
    x2 = x.reshape(B * S, D_MODEL)
    z = project(x2, norm_g, w_in_b)
    cos, sin = rope_tables(pos)
    sel_st, win_st, dsa_st, idx_st, ks, vs, kw, vw, kb, vb, ki = key_post(z, cos, sin, qk_g, B, S)
    kv_cmp = seg(z, 'kv_c').reshape(B, S, 2, G_A, HD)
    kc, vc = compress(kv_cmp, w_cmp, pe_cmp, qk_g[1])
    pad_c = lambda t: jnp.pad(t, ((0, 0), (0, LANES - t.shape[1]), (0, 0), (0, 0))).transpose(0, 2, 1, 3).astype(BF16)
    a_a = nsa_prompt_attention(z, cos, sin, qk_g, pad_c(kc), pad_c(vc), ks, vs, kw, vw, B, S)
    a_b = dsa_prompt_attention(z, cos, sin, qk_g, ki, kb, vb, B, S)
    a_p = pool_prompt(z, w_pool_b, pool_scale, B, S)
    kv_m = mem_kv(mem, mem_norm_g, w_mem_kv_b, qk_g[7])
    kvm_t = kv_m.transpose(2, 0, 3, 1, 4).astype(BF16)
    a_m = mem_prompt_attention(z, qk_g, kvm_t[0], kvm_t[1], B, S)
    y_mid = merge_branches((a_a, a_b, a_p, a_m), z, w_br_b)
    y = out_proj(x2, y_mid, w_out_b).reshape(B, S, D_MODEL)
    wb = min(WINDOW, S)
    u_c = seg(z, 'u_c').reshape(B, S, C_WIDTH)
    return y, (kv_cmp, sel_st.reshape(B, S, 2, G_A, HD), dsa_st.reshape(B, S, 2, HD), idx_st,
               win_st[:, S - wb:].reshape(B, wb, 2, G_A, HD), u_c[:, S - POOL_BUF:], kv_m)


def sample_layer(x, pos, l, page_table, cache_cmp, cache_sel, cache_dsa, cache_idx, win_buf, pool_buf, mem_cache,
                 norm_g, w_in_b, qk_g, w_cmp, pe_cmp, w_pool, pool_scale, w_br_b, w_out_b):
    B, T, _ = x.shape
    past = page_table.shape[1] * PAGE_SIZE
    L = past + T
    x2 = x.reshape(B * T, D_MODEL)
    z2 = project(x2, norm_g, w_in_b)
    f = front_sample(z2.reshape(B, T, N_FRONT), pos, qk_g)
    cmp_rows = jnp.concatenate([gather_pages(cache_cmp, l, page_table), f['kv_cmp']], axis=1)
    kc, vc = compress(cmp_rows, w_cmp, pe_cmp, qk_g[1])
    o_c, p_c = cmp_attend(f['q_n'], pos, kc, vc)
    idx, valid = select_blocks(p_c, pos, -(-L // SEL_LEN))
    kg, vg = gather_sel_sample(cache_sel, l, page_table, f['kv_sel'], idx, past)
    o_s = sel_attend(f['q_r'], pos, kg, vg, idx, valid)
    wb = win_buf.shape[1]
    win_rows = jnp.concatenate([win_buf, f['kv_win']], axis=1)
    pos_k = past - wb + jnp.arange(wb + T, dtype=jnp.int32)
    o_w = win_attend(f['q_r'], pos, win_rows[:, :, 0], win_rows[:, :, 1], pos_k)
    o_a = nsa_combine(f['g_a'], o_c, o_s, o_w)
    k_i_all = jnp.concatenate([gather_pages(cache_idx, l, page_table), f['k_i']], axis=1)
    idx_b, valid_b = dsa_select(f['q_i'], f['w_i'], k_i_all, pos, jnp.arange(L, dtype=jnp.int32),
                                min(DSA_TOPK, L // 4))
    kvg = gather_rows_sample(cache_dsa, l, page_table, f['kv_b'], idx_b, past)
    o_b = dsa_attend(f['q_b'], kvg, valid_b)
    u_ext = jnp.concatenate([pool_buf, f['u_c']], axis=1)
    o_p = pool_mix(u_ext, pos, w_pool, pool_scale)
    o_m = mem_attend(f['q_m'], mem_cache)
    acts = tuple((o * jax.nn.silu(s)).reshape(B * T, -1).astype(BF16)
                 for o, s in zip((o_a, o_b, o_p, o_m), f['silu']))
    y_mid = merge_branches(acts, z2, w_br_b)
    y = out_proj(x2, y_mid, w_out_b).reshape(B, T, D_MODEL)
    return y, (f['kv_cmp'], f['kv_sel'], f['kv_b'], f['k_i'], win_rows[:, T:], u_ext[:, T:])


def kernel(x_prompt, x_sample, mem_prompt, cache_cmp, cache_sel, cache_dsa, cache_idx, state_win, state_pool,
           cache_mem, page_table, norm_g, w_in, qk_g, w_cmp, pe_cmp, w_pool, pool_scale, mem_norm_g, w_mem_kv,
           w_br, w_out):
    past = page_table.shape[1] * PAGE_SIZE
    pos_p = jnp.arange(x_prompt.shape[1], dtype=jnp.int32)
    pos_s = past + jnp.arange(x_sample.shape[1], dtype=jnp.int32)
    xp, xs = x_prompt, x_sample
    st_p, st_s = [], []
    for l in range(DEPTH):
        w_in_b = prep_w_in(w_in[l])
        w_br_b = w_br[l].astype(BF16)
        w_out_b = w_out[l].astype(BF16)
        w_mem_kv_b = w_mem_kv[l].astype(BF16)
        xp, sp = prompt_layer(xp, mem_prompt, pos_p, norm_g[l], w_in_b, qk_g[l], w_cmp[l], pe_cmp[l],
                              w_pool[l].astype(BF16), pool_scale[l], mem_norm_g[l], w_mem_kv_b, w_br_b, w_out_b)
        xs, ss = sample_layer(xs, pos_s, l, page_table, cache_cmp, cache_sel, cache_dsa, cache_idx, state_win[l],
                              state_pool[l], cache_mem[l], norm_g[l], w_in_b, qk_g[l], w_cmp[l], pe_cmp[l],
                              w_pool[l], pool_scale[l], w_br_b, w_out_b)
        st_p.append(sp)
        st_s.append(ss)

    def stack(states, i):
        return jnp.stack([s[i] for s in states], axis=0)

    return (xp, xs,
            stack(st_p, 0), stack(st_s, 0),
            stack(st_p, 1), stack(st_s, 1),
            stack(st_p, 2), stack(st_s, 2),
            stack(st_p, 3), stack(st_s, 3),
            stack(st_p, 4), stack(st_s, 4),
            stack(st_p, 5), stack(st_s, 5),
            stack(st_p, 6))
```

```python
import functools

import numpy as np
import jax
import jax.numpy as jnp
from jax import lax
from jax.experimental import pallas as pl
from jax.experimental.pallas import tpu as pltpu

D_MODEL = 4096
DEPTH = 2
PAGE_SIZE = 128
HD = 128
H_A = 16
G_A = 2
R_A = H_A // G_A
CMP_LEN = 32
CMP_STRIDE = 16
SEL_LEN = 64
SEL_CHUNKS = SEL_LEN // CMP_STRIDE
N_SEL = 16
WINDOW = 512
H_B = 16
H_I = 32
D_I = 128
DSA_TOPK = 256
C_GROUPS = 4
POOL_WINDOWS = (2, 4, 8, 16)
C_WIDTH = 2048
C_GW = C_WIDTH // C_GROUPS
POOL_BUF = max(POOL_WINDOWS) - 1
H_M = 4
N_BRANCH = 4
W_A = H_A * HD
W_B = H_B * HD
W_M = H_M * HD
W_BR = W_A + W_B + C_WIDTH + W_M
QBLK = 128
ROPE_THETA = 10000.0
EPS = 1e-6
SCALE = HD ** -0.5
NEG = -1e30
FORCE = 1e9
INT_MIN = -2 ** 31

LANES = 128
BF16 = jnp.bfloat16

SRC_NAMES = ('q_a', 'kv_c', 'kv_s', 'kv_w', 'g_a', 's_a', 'q_b', 'kv_b', 'q_i', 'k_i', 'w_i', 's_b',
             'u_c', 's_c', 'q_m', 's_m', 'g_m')
SRC_SIZES = (W_A, 2 * G_A * HD, 2 * G_A * HD, 2 * G_A * HD, 3 * H_A, W_A, W_B, 2 * HD, H_I * D_I, D_I, H_I, W_B,
             C_WIDTH, C_WIDTH, W_M, W_M, N_BRANCH * D_MODEL)
SRC_OFF = dict(zip(SRC_NAMES, np.concatenate([[0], np.cumsum(SRC_SIZES)[:-1]]).tolist()))
SEG_LEN = dict(zip(SRC_NAMES, SRC_SIZES))
DST_ORDER = ('q_a', 's_a', 'q_b', 's_b', 'u_c', 's_c', 'q_i', 'g_m', 'kv_c', 'kv_s', 'kv_w', 'q_m', 's_m', 'kv_b',
             'k_i', 'g_a', 'w_i')
SEG_PAD = {n: -(-SEG_LEN[n] // LANES) * LANES for n in DST_ORDER}
SEG_OFF = dict(zip(DST_ORDER, np.concatenate([[0], np.cumsum([SEG_PAD[n] for n in DST_ORDER])[:-1]]).tolist()))
TN_FRONT = 512
N_FRONT = -(-sum(SEG_PAD.values()) // TN_FRONT) * TN_FRONT
VMEM_LIMIT = 48 * 1024 * 1024

N_CLASS = 4
WIN_KEYS = WINDOW + QBLK
IDX_CHUNK = 512
TB_POST = 256


def _cparams(*sem):
    return pltpu.CompilerParams(dimension_semantics=sem, vmem_limit_bytes=VMEM_LIMIT)


def _blk(name, width):
    assert SEG_OFF[name] % width == 0
    return SEG_OFF[name] // width


def _rmsnorm_cast_body(x_ref, g_ref, o_ref):
    x = x_ref[...]
    y = x * lax.rsqrt(jnp.mean(x * x, axis=-1, keepdims=True) + EPS)
    o_ref[...] = (y * g_ref[...]).astype(o_ref.dtype)


def rmsnorm_cast(x, g, tm):
    m, d = x.shape
    return pl.pallas_call(
        _rmsnorm_cast_body,
        grid=(m // tm,),
        in_specs=[pl.BlockSpec((tm, d), lambda i: (i, 0)), pl.BlockSpec((1, d), lambda i: (0, 0))],
        out_specs=pl.BlockSpec((tm, d), lambda i: (i, 0)),
        out_shape=jax.ShapeDtypeStruct((m, d), BF16),
        compiler_params=_cparams("parallel"),
        name="rmsnorm_cast",
    )(x, g.reshape(1, d))


def _matmul_body(a_ref, w_ref, o_ref):
    o_ref[...] = jnp.dot(a_ref[...], w_ref[...], preferred_element_type=jnp.float32)


def matmul(a, w, tm, tn):
    m, k = a.shape
    n = w.shape[1]
    return pl.pallas_call(
        _matmul_body,
        grid=(m // tm, n // tn),
        in_specs=[pl.BlockSpec((tm, k), lambda i, j: (i, 0)), pl.BlockSpec((k, tn), lambda i, j: (0, j))],
        out_specs=pl.BlockSpec((tm, tn), lambda i, j: (i, j)),
        out_shape=jax.ShapeDtypeStruct((m, n), jnp.float32),
        compiler_params=_cparams("parallel", "parallel"),
        name="matmul",
    )(a, w)


def _row_tile(m, big=1024):
    return big if m % big == 0 else m


def _merge_body(aa, ab, ap, am, ga, gb, gp, gm, wa, wb, wp, wm, o_ref):
    acc = jax.nn.sigmoid(ga[...]) * jnp.dot(aa[...], wa[...], preferred_element_type=jnp.float32)
    acc += jax.nn.sigmoid(gb[...]) * jnp.dot(ab[...], wb[...], preferred_element_type=jnp.float32)
    acc += jax.nn.sigmoid(gp[...]) * jnp.dot(ap[...], wp[...], preferred_element_type=jnp.float32)
    acc += jax.nn.sigmoid(gm[...]) * jnp.dot(am[...], wm[...], preferred_element_type=jnp.float32)
    o_ref[...] = acc.astype(o_ref.dtype)


def merge_branches(acts, z, w_br_b):
    m = z.shape[0]
    tm, tn = _row_tile(m, 512), 512
    gate0 = _blk('g_m', tn)
    per = D_MODEL // tn
    widths = (W_A, W_B, C_WIDTH, W_M)
    rows = np.concatenate([[0], np.cumsum(widths)[:-1]]).tolist()
    act_specs = [pl.BlockSpec((tm, w), lambda i, j: (i, 0)) for w in widths]
    gate_specs = [pl.BlockSpec((tm, tn), lambda i, j, b=b: (i, gate0 + b * per + j)) for b in range(N_BRANCH)]
    w_specs = [pl.BlockSpec((w, tn), lambda i, j, r=r, w=w: (r // w, j)) for w, r in zip(widths, rows)]
    assert all(r % w == 0 for w, r in zip(widths, rows))
    return pl.pallas_call(
        _merge_body,
        grid=(m // tm, D_MODEL // tn),
        in_specs=act_specs + gate_specs + w_specs,
        out_specs=pl.BlockSpec((tm, tn), lambda i, j: (i, j)),
        out_shape=jax.ShapeDtypeStruct((m, D_MODEL), BF16),
        compiler_params=_cparams("parallel", "arbitrary"),
        name="merge_branches",
    )(*acts, z, z, z, z, w_br_b, w_br_b, w_br_b, w_br_b)


def _out_body(x_ref, y_ref, w_ref, o_ref):
    o_ref[...] = x_ref[...] + jnp.dot(y_ref[...], w_ref[...], preferred_element_type=jnp.float32)


def out_proj(x2, y_mid, w_out_b):
    m = x2.shape[0]
    tm, tn = _row_tile(m), 512
    return pl.pallas_call(
        _out_body,
        grid=(m // tm, D_MODEL // tn),
        in_specs=[pl.BlockSpec((tm, tn), lambda i, j: (i, j)), pl.BlockSpec((tm, D_MODEL), lambda i, j: (i, 0)),
                  pl.BlockSpec((D_MODEL, tn), lambda i, j: (0, j))],
        out_specs=pl.BlockSpec((tm, tn), lambda i, j: (i, j)),
        out_shape=jax.ShapeDtypeStruct((m, D_MODEL), jnp.float32),
        compiler_params=_cparams("parallel", "arbitrary"),
        name="out_proj",
    )(x2, y_mid, w_out_b)


def _dot_nt(a, b):
    return lax.dot_general(a, b, (((1,), (1,)), ((), ())), preferred_element_type=jnp.float32)


def _rms(x, g):
    return x * lax.rsqrt(jnp.mean(x * x, axis=-1, keepdims=True) + EPS) * g


def _rope(y, cos, sin):
    return y * cos + pltpu.roll(y, HD // 2, 1) * sin


def _silu(s):
    return s * jax.nn.sigmoid(s)


def _attend(q, k, v, bias):
    s = _dot_nt(q, k) * SCALE + bias
    e = jnp.exp(s - jnp.max(s, axis=1, keepdims=True))
    d = jnp.sum(e, axis=1, keepdims=True)
    return jnp.dot(e.astype(BF16), v, preferred_element_type=jnp.float32) * (1.0 / d)


def _kpost_body(zs_ref, zw_ref, zb_ref, zi_ref, cos_ref, sin_ref, g_ref,
                sel_o, win_o, dsa_o, idx_o, ks_o, vs_o, kw_o, vw_o, kb_o, vb_o, ki_o):
    cos = cos_ref[...]
    sin = sin_ref[...]
    for z_ref, st_o, k_o, v_o, gi in ((zs_ref, sel_o, ks_o, vs_o, 2), (zw_ref, win_o, kw_o, vw_o, 3)):
        for g in range(G_A):
            k = _rope(_rms(z_ref[:, g * HD:(g + 1) * HD], g_ref[gi:gi + 1, :]), cos, sin)
            v = z_ref[:, (G_A + g) * HD:(G_A + g + 1) * HD]
            st_o[:, g * HD:(g + 1) * HD] = k
            st_o[:, (G_A + g) * HD:(G_A + g + 1) * HD] = v
            k_o[g] = k.astype(BF16)
            v_o[g] = v.astype(BF16)
    k = _rope(_rms(zb_ref[:, :HD], g_ref[5:6, :]), cos, sin)
    v = zb_ref[:, HD:]
    dsa_o[:, :HD] = k
    dsa_o[:, HD:] = v
    kb_o[...] = k.astype(BF16)
    vb_o[...] = v.astype(BF16)
    ki = _rope(zi_ref[...], cos, sin)
    idx_o[...] = ki
    ki_o[...] = ki.astype(BF16)


def key_post(z, cos, sin, qk_g, B, S):
    tb = TB_POST
    nb = S // tb
    zmap = lambda name, w: pl.BlockSpec((tb, w), lambda b, j: (b * nb + j, _blk(name, w)))
    row = lambda w: pl.BlockSpec((None, tb, w), lambda b, j: (b, j, 0))
    grp = pl.BlockSpec((None, G_A, tb, HD), lambda b, j: (b, 0, j, 0))
    f32, sds = jnp.float32, jax.ShapeDtypeStruct
    return pl.pallas_call(
        _kpost_body,
        grid=(B, nb),
        in_specs=[zmap('kv_s', 512), zmap('kv_w', 512), zmap('kv_b', 256), zmap('k_i', 128),
                  pl.BlockSpec((tb, HD), lambda b, j: (j, 0)), pl.BlockSpec((tb, HD), lambda b, j: (j, 0)),
                  pl.BlockSpec((8, HD), lambda b, j: (0, 0))],
        out_specs=[row(512), row(512), row(256), row(128), grp, grp, grp, grp, row(HD), row(HD), row(D_I)],
        out_shape=[sds((B, S, 512), f32), sds((B, S, 512), f32), sds((B, S, 256), f32), sds((B, S, D_I), f32),
                   sds((B, G_A, S, HD), BF16), sds((B, G_A, S, HD), BF16), sds((B, G_A, S, HD), BF16),
                   sds((B, G_A, S, HD), BF16), sds((B, S, HD), BF16), sds((B, S, HD), BF16), sds((B, S, D_I), BF16)],
        compiler_params=_cparams("parallel", "parallel"),
        name="key_post",
    )(z, z, z, z, cos, sin, qk_g)


def _nsa_body(zq_ref, zs_ref, ga_ref, cos_ref, sin_ref, g_ref, kc_ref, vc_ref, ks_ref, vs_ref, kw_ref, vw_ref,
              a_ref, e_ref, o_ref, qr_s, acc_s, sel_s, bias_s, wbias_s, *, per):
    j = pl.program_id(1)
    q0 = pl.multiple_of(j * QBLK, QBLK)
    pos = q0 + lax.broadcasted_iota(jnp.int32, (QBLK, 1), 0)
    lane = lax.broadcasted_iota(jnp.int32, (1, LANES), 1)
    gates = jax.nn.sigmoid(ga_ref[...])
    cos = cos_ref[...]
    sin = sin_ref[...]
    n_cmp = kc_ref.shape[1] - 1
    n_blk = e_ref.shape[1] // SEL_LEN

    w0 = pl.multiple_of(jnp.maximum(q0 - WINDOW, 0), QBLK)
    dist = pos - (w0 + lax.broadcasted_iota(jnp.int32, (1, WIN_KEYS), 1))
    wbias_s[...] = jnp.where(dist >= 0, jnp.where(dist < WINDOW, 0.0, NEG), NEG)
    cmask = (lane < n_cmp) & (lane * CMP_STRIDE + (CMP_LEN - 1) <= pos)
    cur = pos // SEL_LEN
    forced = (lane == 0) | (lane == cur) | (lane == cur - 1)

    for g in range(G_A):
        kw = kw_ref[g, pl.ds(w0, WIN_KEYS), :]
        vw = vw_ref[g, pl.ds(w0, WIN_KEYS), :]
        pg = jnp.zeros((QBLK, LANES), jnp.float32)
        for r in range(R_A):
            h = g * R_A + r
            qn = _rms(zq_ref[:, h * HD:(h + 1) * HD], g_ref[0:1, :])
            qr = _rope(qn, cos, sin).astype(BF16)
            qr_s[h] = qr
            s = jnp.where(cmask, _dot_nt(qn.astype(BF16), kc_ref[g]) * SCALE, NEG)
            e = jnp.where(cmask, jnp.exp(s - jnp.max(s, axis=1, keepdims=True)), 0.0)
            p = e / jnp.maximum(jnp.sum(e, axis=1, keepdims=True), 1e-30)
            pg = pg + p
            o_c = jnp.dot(p.astype(BF16), vc_ref[g], preferred_element_type=jnp.float32)
            o_w = _attend(qr, kw, vw, wbias_s[...])
            acc_s[:, h * HD:(h + 1) * HD] = (gates[:, h:h + 1] * o_c
                                            + gates[:, 2 * H_A + h:2 * H_A + h + 1] * o_w)

        hi = pg.astype(BF16)
        r1 = pg - hi.astype(jnp.float32)
        mid = r1.astype(BF16)
        lo = (r1 - mid.astype(jnp.float32)).astype(BF16)
        a = a_ref[...]
        score = (jnp.dot(hi, a, preferred_element_type=jnp.float32)
                 + jnp.dot(mid, a, preferred_element_type=jnp.float32)
                 + jnp.dot(lo, a, preferred_element_type=jnp.float32))
        score = jnp.where(forced, FORCE, jnp.where(lane <= cur, score, NEG))
        score = jnp.where(lane < n_blk, score, -3e38)
        rank = jnp.zeros((QBLK, LANES), jnp.int32)
        for jp in range(n_blk):
            col = score[:, jp:jp + 1]
            rank = rank + jnp.where(col > score, 1, jnp.where(col == score, jnp.where(lane > jp, 1, 0), 0))
        sel_s[g] = jnp.where(rank < min(N_SEL, n_blk), jnp.where(score > 0.5 * NEG, 1.0, 0.0), 0.0)

    def selected(nk):
        kidx = lax.broadcasted_iota(jnp.int32, (1, nk), 1)
        for g in range(G_A):
            selk = jnp.dot(sel_s[g].astype(BF16), e_ref[:, :nk], preferred_element_type=jnp.float32)
            bias_s[:, :nk] = jnp.where(kidx <= pos, jnp.where(selk > 0.5, 0.0, NEG), NEG)
            for r in range(R_A):
                h = g * R_A + r
                o_s = _attend(qr_s[h], ks_ref[g, :nk, :], vs_ref[g, :nk, :], bias_s[:, :nk])
                o = acc_s[:, h * HD:(h + 1) * HD] + gates[:, H_A + h:H_A + h + 1] * o_s
                o_ref[:, h * HD:(h + 1) * HD] = (o * _silu(zs_ref[:, h * HD:(h + 1) * HD])).astype(o_ref.dtype)

    for c in range(N_CLASS):
        pl.when(j // per == c)(functools.partial(selected, (c + 1) * per * QBLK))


def _sel_matrices(n_cmp, n_blk):
    a = np.zeros((LANES, LANES), np.float32)
    for i in range(n_cmp):
        for c in (i, i + 1):
            if c // SEL_CHUNKS < n_blk:
                a[i, c // SEL_CHUNKS] += 1.0
    e = np.zeros((LANES, n_blk * SEL_LEN), np.float32)
    for j in range(n_blk):
        e[j, j * SEL_LEN:(j + 1) * SEL_LEN] = 1.0
    return jnp.asarray(a, BF16), jnp.asarray(e, BF16)


def nsa_prompt_attention(z, cos, sin, qk_g, kc, vc, ks, vs, kw, vw, B, S):
    nqb = S // QBLK
    assert nqb % N_CLASS == 0 and S >= WIN_KEYS and S // CMP_STRIDE - 1 < LANES
    a_mat, e_mat = _sel_matrices(S // CMP_STRIDE - 1, S // SEL_LEN)
    zmap = lambda name, w: pl.BlockSpec((QBLK, w), lambda b, j: (b * nqb + j, _blk(name, w)))
    tab = pl.BlockSpec((QBLK, HD), lambda b, j: (j, 0))
    per_b = lambda n: pl.BlockSpec((None, G_A, n, HD), lambda b, j: (b, 0, 0, 0))
    return pl.pallas_call(
        functools.partial(_nsa_body, per=nqb // N_CLASS),
        grid=(B, nqb),
        in_specs=[zmap('q_a', W_A), zmap('s_a', W_A), zmap('g_a', LANES), tab, tab,
                  pl.BlockSpec((8, HD), lambda b, j: (0, 0)),
                  per_b(LANES), per_b(LANES), per_b(S), per_b(S), per_b(S), per_b(S),
                  pl.BlockSpec((LANES, LANES), lambda b, j: (0, 0)), pl.BlockSpec((LANES, S), lambda b, j: (0, 0))],
        out_specs=pl.BlockSpec((QBLK, W_A), lambda b, j: (b * nqb + j, 0)),
        out_shape=jax.ShapeDtypeStruct((B * S, W_A), BF16),
        scratch_shapes=[pltpu.VMEM((H_A, QBLK, HD), BF16), pltpu.VMEM((QBLK, W_A), jnp.float32),
                        pltpu.VMEM((G_A, QBLK, LANES), jnp.float32), pltpu.VMEM((QBLK, S), jnp.float32),
                        pltpu.VMEM((QBLK, WIN_KEYS), jnp.float32)],
        compiler_params=_cparams("parallel", "arbitrary"),
        name="nsa_prompt",
    )(z, z, z, cos, sin, qk_g, kc, vc, ks, vs, kw, vw, a_mat, e_mat)


def _topk_bias(score_s, key_s, bias_s, n, k_top):
    rows = score_s.shape[0]
    kidx = lax.broadcasted_iota(jnp.int32, (1, n), 1)
    bits = pltpu.bitcast(score_s[:, :n], jnp.int32)
    key_s[:, :n] = jnp.where(bits >= 0, bits, bits ^ 0x7FFFFFFF)

    def value_step(it, ans):
        cand = ans | jnp.left_shift(jnp.int32(1), 31 - it)
        cnt = jnp.sum(jnp.where(key_s[:, :n] >= (cand ^ INT_MIN), 1, 0), axis=1, keepdims=True)
        return jnp.where(cnt >= k_top, cand, ans)

    thr = lax.fori_loop(0, 32, value_step, jnp.zeros((rows, 1), jnp.int32)) ^ INT_MIN
    need = k_top - jnp.sum(jnp.where(key_s[:, :n] > thr, 1, 0), axis=1, keepdims=True)
    idx_bits = max(1, (n - 1).bit_length())

    def index_step(it, ans):
        cand = ans | jnp.left_shift(jnp.int32(1), idx_bits - 1 - it)
        below = jnp.where(key_s[:, :n] == thr, jnp.where(kidx < cand, 1, 0), 0)
        return jnp.where(jnp.sum(below, axis=1, keepdims=True) < need, cand, ans)

    cut = lax.fori_loop(0, idx_bits, index_step, jnp.zeros((rows, 1), jnp.int32))
    key = key_s[:, :n]
    chosen = jnp.where(key > thr, 1, jnp.where(key == thr, jnp.where(kidx <= cut, 1, 0), 0))
    bias_s[:, :n] = jnp.where(chosen > 0, jnp.where(score_s[:, :n] > 0.5 * NEG, 0.0, NEG), NEG)


def _dsa_body(zqi_ref, wi_ref, zqb_ref, zs_ref, cos_ref, sin_ref, g_ref, ki_ref, kb_ref, vb_ref, o_ref,
              qi_s, score_s, key_s, bias_s, *, nk, qb0, k_top):
    pos = (pl.program_id(1) + qb0) * QBLK + lax.broadcasted_iota(jnp.int32, (QBLK, 1), 0)
    kidx = lax.broadcasted_iota(jnp.int32, (1, nk), 1)
    cos = cos_ref[...]
    sin = sin_ref[...]
    for h in range(H_I):
        qi_s[h] = _rope(zqi_ref[:, h * D_I:(h + 1) * D_I], cos, sin).astype(BF16)

    for c in range(nk // IDX_CHUNK):
        k = ki_ref[c * IDX_CHUNK:(c + 1) * IDX_CHUNK, :]
        acc = jnp.zeros((QBLK, IDX_CHUNK), jnp.float32)
        for h in range(H_I):
            acc = acc + wi_ref[:, h:h + 1] * jnp.maximum(_dot_nt(qi_s[h], k), 0.0)
        score_s[:, c * IDX_CHUNK:(c + 1) * IDX_CHUNK] = acc * ((D_I ** -0.5) * (H_I ** -0.5))
    score_s[...] = jnp.where(kidx <= pos, score_s[...], NEG)
    _topk_bias(score_s, key_s, bias_s, nk, k_top)

    for h in range(H_B):
        cols = slice(h * HD, (h + 1) * HD)
        q = _rope(_rms(zqb_ref[:, cols], g_ref[4:5, :]), cos, sin).astype(BF16)
        o = _attend(q, kb_ref[:nk, :], vb_ref[:nk, :], bias_s[...])
        o_ref[:, cols] = (o * _silu(zs_ref[:, cols])).astype(o_ref.dtype)


def dsa_prompt_attention(z, cos, sin, qk_g, ki, kb, vb, B, S):
    nqb = S // QBLK
    per = nqb // N_CLASS
    assert nqb % N_CLASS == 0 and (per * QBLK) % IDX_CHUNK == 0
    tab = lambda c: pl.BlockSpec((QBLK, HD), lambda b, j: (c * per + j, 0))
    per_b = lambda d: pl.BlockSpec((None, S, d), lambda b, j: (b, 0, 0))
    outs = []
    for c in range(N_CLASS):
        nk = (c + 1) * per * QBLK
        zmap = lambda name, w, c=c: pl.BlockSpec((QBLK, w), lambda b, j: (b * nqb + c * per + j, _blk(name, w)))
        outs.append(pl.pallas_call(
            functools.partial(_dsa_body, nk=nk, qb0=c * per, k_top=min(DSA_TOPK, S // 4)),
            grid=(B, per),
            in_specs=[zmap('q_i', H_I * D_I), zmap('w_i', LANES), zmap('q_b', W_B), zmap('s_b', W_B), tab(c), tab(c),
                      pl.BlockSpec((8, HD), lambda b, j: (0, 0)), per_b(D_I), per_b(HD), per_b(HD)],
            out_specs=pl.BlockSpec((None, QBLK, W_B), lambda b, j: (b, j, 0)),
            out_shape=jax.ShapeDtypeStruct((B, per * QBLK, W_B), BF16),
            scratch_shapes=[pltpu.VMEM((H_I, QBLK, D_I), BF16), pltpu.VMEM((QBLK, nk), jnp.float32),
                            pltpu.VMEM((QBLK, nk), jnp.int32), pltpu.VMEM((QBLK, nk), jnp.float32)],
            compiler_params=_cparams("parallel", "arbitrary"),
            name=f"dsa_prompt_{nk}",
        )(z, z, z, z, cos, sin, qk_g, ki, kb, vb))
    return jnp.concatenate(outs, axis=1).reshape(B * S, W_B)


def _topk_mask_body(s_ref, o_ref, key_s, *, k_top):
    _topk_bias(s_ref, key_s, o_ref, s_ref.shape[1], k_top)


def topk_bias(score, k_top):
    r, n = score.shape
    return pl.pallas_call(
        functools.partial(_topk_mask_body, k_top=k_top),
        out_shape=jax.ShapeDtypeStruct((r, n), jnp.float32),
        scratch_shapes=[pltpu.VMEM((r, n), jnp.int32)],
        compiler_params=pltpu.CompilerParams(vmem_limit_bytes=VMEM_LIMIT),
        name="topk_bias",
    )(score)


def _pool_body(u_ref, up_ref, s_ref, w_ref, sc_ref, o_ref, ext_s):
    j = pl.program_id(1)
    tb = u_ref.shape[0]
    halo = POOL_BUF + 1
    prev = up_ref[tb - halo:, :]
    ext_s[:halo, :] = jnp.where(j > 0, prev, 0.0)
    ext_s[halo:, :] = u_ref[...]
    pos = j * tb + lax.broadcasted_iota(jnp.int32, (tb, 1), 0)
    for gi, w in enumerate(POOL_WINDOWS):
        cols = slice(gi * C_GW, (gi + 1) * C_GW)
        u = u_ref[:, cols]
        tot = u
        for k in range(1, w):
            tot = tot + ext_s[halo - k:halo - k + tb, cols]
        cnt = jnp.minimum(pos + 1, w).astype(jnp.float32)
        d = tot / cnt - u
        y = jnp.dot(d.astype(BF16), w_ref[gi], preferred_element_type=jnp.float32) * sc_ref[:, cols]
        o_ref[:, cols] = (y * _silu(s_ref[:, cols])).astype(o_ref.dtype)


def pool_prompt(z, w_pool_b, pool_scale, B, S):
    tb = TB_POST
    nb = S // tb
    ublk = _blk('u_c', C_WIDTH)
    return pl.pallas_call(
        _pool_body,
        grid=(B, nb),
        in_specs=[pl.BlockSpec((tb, C_WIDTH), lambda b, j: (b * nb + j, ublk)),
                  pl.BlockSpec((tb, C_WIDTH), lambda b, j: (b * nb + jnp.maximum(j - 1, 0), ublk)),
                  pl.BlockSpec((tb, C_WIDTH), lambda b, j: (b * nb + j, _blk('s_c', C_WIDTH))),
                  pl.BlockSpec((C_GROUPS, C_GW, C_GW), lambda b, j: (0, 0, 0)),
                  pl.BlockSpec((1, C_WIDTH), lambda b, j: (0, 0))],
        out_specs=pl.BlockSpec((tb, C_WIDTH), lambda b, j: (b * nb + j, 0)),
        out_shape=jax.ShapeDtypeStruct((B * S, C_WIDTH), BF16),
        scratch_shapes=[pltpu.VMEM((tb + POOL_BUF + 1, C_WIDTH), jnp.float32)],
        compiler_params=_cparams("parallel", "arbitrary"),
        name="pool_prompt",
    )(z, z, z, w_pool_b, pool_scale.reshape(1, C_WIDTH))


def _mem_body(zq_ref, zs_ref, g_ref, k_ref, v_ref, o_ref):
    zero = jnp.zeros((1, k_ref.shape[1]), jnp.float32)
    for h in range(H_M):
        cols = slice(h * HD, (h + 1) * HD)
        q = _rms(zq_ref[:, cols], g_ref[6:7, :]).astype(BF16)
        o = _attend(q, k_ref[h], v_ref[h], zero)
        o_ref[:, cols] = (o * _silu(zs_ref[:, cols])).astype(o_ref.dtype)


def mem_prompt_attention(z, qk_g, km, vm, B, S):
    tb = 2 * TB_POST
    nb = S // tb
    n_mem = km.shape[2]
    return pl.pallas_call(
        _mem_body,
        grid=(B, nb),
        in_specs=[pl.BlockSpec((tb, W_M), lambda b, j: (b * nb + j, _blk('q_m', W_M))),
                  pl.BlockSpec((tb, W_M), lambda b, j: (b * nb + j, _blk('s_m', W_M))),
                  pl.BlockSpec((8, HD), lambda b, j: (0, 0)),
                  pl.BlockSpec((None, H_M, n_mem, HD), lambda b, j: (b, 0, 0, 0)),
                  pl.BlockSpec((None, H_M, n_mem, HD), lambda b, j: (b, 0, 0, 0))],
        out_specs=pl.BlockSpec((tb, W_M), lambda b, j: (b * nb + j, 0)),
        out_shape=jax.ShapeDtypeStruct((B * S, W_M), BF16),
        compiler_params=_cparams("parallel", "parallel"),
        name="mem_prompt",
    )(z, z, qk_g, km, vm)


def prep_w_in(w):
    parts = []
    for name in DST_ORDER:
        p = w[:, SRC_OFF[name]:SRC_OFF[name] + SEG_LEN[name]].astype(BF16)
        if SEG_PAD[name] > SEG_LEN[name]:
            p = jnp.pad(p, ((0, 0), (0, SEG_PAD[name] - SEG_LEN[name])))
        parts.append(p)
    tail = N_FRONT - sum(SEG_PAD.values())
    if tail:
        parts.append(jnp.zeros((w.shape[0], tail), BF16))
    return jnp.concatenate(parts, axis=1)


def seg(z, name):
    return z[..., SEG_OFF[name]:SEG_OFF[name] + SEG_LEN[name]]


def rope_tables(pos):
    half = HD // 2
    inv_freq = ROPE_THETA ** (-jnp.arange(half, dtype=jnp.float32) / half)
    ang = pos.astype(jnp.float32)[:, None] * inv_freq[None, :]
    c, s = jnp.cos(ang), jnp.sin(ang)
    return jnp.concatenate([c, c], axis=-1), jnp.concatenate([-s, s], axis=-1)


def project(x2, norm_g, w_in_b):
    m = x2.shape[0]
    h = rmsnorm_cast(x2, norm_g, _row_tile(m, 512))
    return matmul(h, w_in_b, _row_tile(m), TN_FRONT)


def rmsnorm(x, g):
    y = x * lax.rsqrt(jnp.mean(x * x, axis=-1, keepdims=True) + EPS)
    return y * g


def rope(x, pos):
    half = x.shape[-1] // 2
    inv_freq = ROPE_THETA ** (-jnp.arange(half, dtype=jnp.float32) / half)
    ang = pos.astype(jnp.float32)[:, None] * inv_freq[None, :]
    cos = jnp.cos(ang)[:, None, :]
    sin = jnp.sin(ang)[:, None, :]
    x1, x2 = x[..., :half], x[..., half:]
    return jnp.concatenate([x1 * cos - x2 * sin, x2 * cos + x1 * sin], axis=-1)


def masked_softmax(s, mask):
    s = jnp.where(mask, s, NEG)
    return jnp.where(mask, jax.nn.softmax(s, axis=-1), 0.0)


def front_sample(z, pos, qk_g):
    B, T, _ = z.shape
    q_n = rmsnorm(seg(z, 'q_a').reshape(B, T, H_A, HD), qk_g[0])
    kv_s = seg(z, 'kv_s').reshape(B, T, 2, G_A, HD)
    kv_w = seg(z, 'kv_w').reshape(B, T, 2, G_A, HD)
    kv_b = seg(z, 'kv_b').reshape(B, T, 2, HD)
    k_b = rope(rmsnorm(kv_b[:, :, 0], qk_g[5])[:, :, None], pos)[:, :, 0]
    return {
        'q_n': q_n,
        'q_r': rope(q_n, pos),
        'kv_cmp': seg(z, 'kv_c').reshape(B, T, 2, G_A, HD),
        'kv_sel': jnp.stack([rope(rmsnorm(kv_s[:, :, 0], qk_g[2]), pos), kv_s[:, :, 1]], axis=2),
        'kv_win': jnp.stack([rope(rmsnorm(kv_w[:, :, 0], qk_g[3]), pos), kv_w[:, :, 1]], axis=2),
        'g_a': seg(z, 'g_a').reshape(B, T, 3, H_A),
        'q_b': rope(rmsnorm(seg(z, 'q_b').reshape(B, T, H_B, HD), qk_g[4]), pos),
        'kv_b': jnp.stack([k_b, kv_b[:, :, 1]], axis=2),
        'q_i': rope(seg(z, 'q_i').reshape(B, T, H_I, D_I), pos),
        'k_i': rope(seg(z, 'k_i')[:, :, None], pos)[:, :, 0],
        'w_i': seg(z, 'w_i'),
        'u_c': seg(z, 'u_c'),
        'q_m': rmsnorm(seg(z, 'q_m').reshape(B, T, H_M, HD), qk_g[6]),
        'silu': (seg(z, 's_a'), seg(z, 's_b'), seg(z, 's_c'), seg(z, 's_m')),
    }


def compress(rows, w_cmp, pe_cmp, g_k):
    B, L = rows.shape[:2]
    n_ch = L // CMP_STRIDE
    ch = rows[:, :n_ch * CMP_STRIDE].reshape(B, n_ch, CMP_STRIDE, 2, G_A, HD)
    ch = jnp.moveaxis(ch, 3, 0)
    pe = pe_cmp[:, None, None]
    c = (jnp.einsum('kbncgd,kgcde->kbnge', ch[:, :, :-1] + pe[:, :, :, :CMP_STRIDE], w_cmp[:, :, :CMP_STRIDE])
         + jnp.einsum('kbncgd,kgcde->kbnge', ch[:, :, 1:] + pe[:, :, :, CMP_STRIDE:], w_cmp[:, :, CMP_STRIDE:]))
    return rmsnorm(c[0], g_k), c[1]


def cmp_attend(q_n, pos_q, kc, vc):
    B, T = q_n.shape[:2]
    qg = q_n.reshape(B, T, G_A, R_A, HD)
    s = jnp.einsum('btgrd,bngd->btgrn', qg, kc) * SCALE
    blk_end = jnp.arange(kc.shape[1], dtype=jnp.int32) * CMP_STRIDE + (CMP_LEN - 1)
    mask = blk_end[None, :] <= pos_q[:, None]
    p = masked_softmax(s, mask[None, :, None, None, :])
    o = jnp.einsum('btgrn,bngd->btgrd', p, vc)
    return o.reshape(B, T, H_A, HD), p


def select_blocks(p_cmp, pos_q, n_blk):
    pg = p_cmp.sum(axis=3)
    zero = jnp.zeros(pg.shape[:-1] + (1,), pg.dtype)
    chunk = jnp.concatenate([pg, zero], -1) + jnp.concatenate([zero, pg], -1)
    chunk = jnp.pad(chunk, ((0, 0), (0, 0), (0, 0), (0, n_blk * SEL_CHUNKS - chunk.shape[-1])))
    score = chunk.reshape(chunk.shape[:-1] + (n_blk, SEL_CHUNKS)).sum(-1)
    j = jnp.arange(n_blk, dtype=jnp.int32)[None, :]
    cur = (pos_q // SEL_LEN)[:, None]
    forced = (j == 0) | (j == cur) | (j == cur - 1)
    valid = j <= cur
    score = jnp.where(forced[None, :, None], FORCE, jnp.where(valid[None, :, None], score, NEG))
    top, idx = lax.top_k(score, min(N_SEL, n_blk))
    return idx, top > 0.5 * NEG


def sel_attend_dense(q_r, pos_q, k, v, idx, valid):
    B, T = q_r.shape[:2]
    L = k.shape[1]
    n_blk = -(-L // SEL_LEN)
    blk = jnp.any((idx[..., None] == jnp.arange(n_blk, dtype=jnp.int32)) & valid[..., None], axis=-2)
    kpos = jnp.arange(L, dtype=jnp.int32)
    mask = jnp.repeat(blk, SEL_LEN, axis=-1)[..., :L] & (kpos[None, :] <= pos_q[:, None])[None, :, None, :]
    qg = q_r.reshape(B, T, G_A, R_A, HD)
    s = jnp.einsum('btgrd,bsgd->btgrs', qg, k) * SCALE
    p = masked_softmax(s, mask[:, :, :, None, :])
    o = jnp.einsum('btgrs,bsgd->btgrd', p, v)
    return o.reshape(B, T, H_A, HD)


def win_attend(q_r, pos_q, kw, vw, pos_k):
    B, T = q_r.shape[:2]
    qg = q_r.reshape(B, T, G_A, R_A, HD)
    s = jnp.einsum('btgrd,bsgd->btgrs', qg, kw) * SCALE
    d = pos_q[:, None] - pos_k[None, :]
    mask = (d >= 0) & (d < WINDOW) & (pos_k[None, :] >= 0)
    p = masked_softmax(s, mask[None, :, None, None, :])
    o = jnp.einsum('btgrs,bsgd->btgrd', p, vw)
    return o.reshape(B, T, H_A, HD)


def nsa_combine(g_a, o_c, o_s, o_w):
    g = jax.nn.sigmoid(g_a)[..., None]
    o = g[:, :, 0] * o_c + g[:, :, 1] * o_s + g[:, :, 2] * o_w
    return o.reshape(o.shape[0], o.shape[1], W_A)


def dsa_select_bias(q_i, w_i, k_i, pos_q, pos_k, k_top):
    B, T = q_i.shape[:2]
    L = k_i.shape[1]
    logits = jnp.einsum('bthd,bsd->bths', q_i, k_i) * (D_I ** -0.5)
    score = jnp.einsum('bths,bth->bts', jax.nn.relu(logits), w_i) * (H_I ** -0.5)
    score = jnp.where((pos_k[None, :] <= pos_q[:, None])[None], score, NEG)
    lp = -(-L // LANES) * LANES
    score = jnp.pad(score, ((0, 0), (0, 0), (0, lp - L)), constant_values=NEG)
    return topk_bias(score.reshape(B * T, lp), k_top).reshape(B, T, lp)[:, :, :L]


def dsa_attend_dense(q_b, k, v, bias):
    B, T = q_b.shape[:2]
    s = jnp.einsum('bthd,bsd->bths', q_b, k) * SCALE
    p = masked_softmax(s, (bias > 0.5 * NEG)[:, :, None, :])
    o = jnp.einsum('bths,bsd->bthd', p, v)
    return o.reshape(B, T, W_B)


def pool_mix(u_ext, pos_q, w_pool, pool_scale):
    B, n_ext, _ = u_ext.shape
    T = n_ext - POOL_BUF
    cs = jnp.concatenate([jnp.zeros((B, 1, C_WIDTH), jnp.float32), jnp.cumsum(u_ext, axis=1)], axis=1)
    end = cs[:, POOL_BUF + 1:]
    means = []
    for gi, w in enumerate(POOL_WINDOWS):
        ch = slice(gi * C_GW, (gi + 1) * C_GW)
        start = cs[:, POOL_BUF + 1 - w: POOL_BUF + 1 - w + T, ch]
        cnt = jnp.minimum(pos_q + 1, w).astype(jnp.float32)[None, :, None]
        means.append((end[:, :, ch] - start) / cnt)
    d = jnp.concatenate(means, axis=-1) - u_ext[:, POOL_BUF:]
    y = jnp.einsum('btgc,gce->btge', d.reshape(B, T, C_GROUPS, C_GW), w_pool)
    return y.reshape(B, T, C_WIDTH) * pool_scale


def mem_kv(mem, mem_norm_g, w_mem_kv_b, g_k):
    B, M, _ = mem.shape
    h = rmsnorm_cast(mem.reshape(B * M, D_MODEL), mem_norm_g, _row_tile(B * M, 512))
    kv = matmul(h, w_mem_kv_b, _row_tile(B * M), 512).reshape(B, M, 2, H_M, HD)
    return jnp.stack([rmsnorm(kv[:, :, 0], g_k), kv[:, :, 1]], axis=2)


def mem_attend(q_m, kv):
    B, T = q_m.shape[:2]
    s = jnp.einsum('bthd,bmhd->bthm', q_m, kv[:, :, 0]) * SCALE
    p = jax.nn.softmax(s, axis=-1)
    o = jnp.einsum('bthm,bmhd->bthd', p, kv[:, :, 1])
    return o.reshape(B, T, W_M)


def gather_pages(pool, l, page_table):
    rows = pool[l, page_table]
    return rows.reshape((page_table.shape[0], -1) + rows.shape[3:])


def prompt_layer(x, mem, pos, norm_g, w_in_b, qk_g, w_cmp, pe_cmp, w_pool_b, pool_scale, mem_norm_g, w_mem_kv_b,
                 w_br_b, w_out_b):
    B, S, _ = x.shape
    x2 = x.reshape(B * S, D_MODEL)
    z = project(x2, norm_g, w_in_b)
    cos, sin = rope_tables(pos)
    sel_st, win_st, dsa_st, idx_st, ks, vs, kw, vw, kb, vb, ki = key_post(z, cos, sin, qk_g, B, S)
    kv_cmp = seg(z, 'kv_c').reshape(B, S, 2, G_A, HD)
    kc, vc = compress(kv_cmp, w_cmp, pe_cmp, qk_g[1])
    pad_c = lambda t: jnp.pad(t, ((0, 0), (0, LANES - t.shape[1]), (0, 0), (0, 0))).transpose(0, 2, 1, 3).astype(BF16)
    a_a = nsa_prompt_attention(z, cos, sin, qk_g, pad_c(kc), pad_c(vc), ks, vs, kw, vw, B, S)
    a_b = dsa_prompt_attention(z, cos, sin, qk_g, ki, kb, vb, B, S)
    a_p = pool_prompt(z, w_pool_b, pool_scale, B, S)
    kv_m = mem_kv(mem, mem_norm_g, w_mem_kv_b, qk_g[7])
    kvm_t = kv_m.transpose(2, 0, 3, 1, 4).astype(BF16)
    a_m = mem_prompt_attention(z, qk_g, kvm_t[0], kvm_t[1], B, S)
    y_mid = merge_branches((a_a, a_b, a_p, a_m), z, w_br_b)
    y = out_proj(x2, y_mid, w_out_b).reshape(B, S, D_MODEL)
    wb = min(WINDOW, S)
    u_c = seg(z, 'u_c').reshape(B, S, C_WIDTH)
    return y, (kv_cmp, sel_st.reshape(B, S, 2, G_A, HD), dsa_st.reshape(B, S, 2, HD), idx_st,
               win_st[:, S - wb:].reshape(B, wb, 2, G_A, HD), u_c[:, S - POOL_BUF:], kv_m)


def sample_layer(x, pos, l, page_table, cache_cmp, cache_sel, cache_dsa, cache_idx, win_buf, pool_buf, mem_cache,
                 norm_g, w_in_b, qk_g, w_cmp, pe_cmp, w_pool, pool_scale, w_br_b, w_out_b):
    B, T, _ = x.shape
    past = page_table.shape[1] * PAGE_SIZE
    L = past + T
    x2 = x.reshape(B * T, D_MODEL)
    z2 = project(x2, norm_g, w_in_b)
    f = front_sample(z2.reshape(B, T, N_FRONT), pos, qk_g)
    cmp_rows = jnp.concatenate([gather_pages(cache_cmp, l, page_table), f['kv_cmp']], axis=1)
    kc, vc = compress(cmp_rows, w_cmp, pe_cmp, qk_g[1])
    o_c, p_c = cmp_attend(f['q_n'], pos, kc, vc)
    idx, valid = select_blocks(p_c, pos, -(-L // SEL_LEN))
    sel_rows = jnp.concatenate([gather_pages(cache_sel, l, page_table), f['kv_sel']], axis=1)
    o_s = sel_attend_dense(f['q_r'], pos, sel_rows[:, :, 0], sel_rows[:, :, 1], idx, valid)
    wb = win_buf.shape[1]
    win_rows = jnp.concatenate([win_buf, f['kv_win']], axis=1)
    pos_k = past - wb + jnp.arange(wb + T, dtype=jnp.int32)
    o_w = win_attend(f['q_r'], pos, win_rows[:, :, 0], win_rows[:, :, 1], pos_k)
    o_a = nsa_combine(f['g_a'], o_c, o_s, o_w)
    k_i_all = jnp.concatenate([gather_pages(cache_idx, l, page_table), f['k_i']], axis=1)
    bias_b = dsa_select_bias(f['q_i'], f['w_i'], k_i_all, pos, jnp.arange(L, dtype=jnp.int32),
                             min(DSA_TOPK, L // 4))
    kv_all = jnp.concatenate([gather_pages(cache_dsa, l, page_table), f['kv_b']], axis=1)
    o_b = dsa_attend_dense(f['q_b'], kv_all[:, :, 0], kv_all[:, :, 1], bias_b)
    u_ext = jnp.concatenate([pool_buf, f['u_c']], axis=1)
    o_p = pool_mix(u_ext, pos, w_pool, pool_scale)
    o_m = mem_attend(f['q_m'], mem_cache)
    acts = tuple((o * jax.nn.silu(s)).reshape(B * T, -1).astype(BF16)
                 for o, s in zip((o_a, o_b, o_p, o_m), f['silu']))
    y_mid = merge_branches(acts, z2, w_br_b)
    y = out_proj(x2, y_mid, w_out_b).reshape(B, T, D_MODEL)
    return y, (f['kv_cmp'], f['kv_sel'], f['kv_b'], f['k_i'], win_rows[:, T:], u_ext[:, T:])


def kernel(x_prompt, x_sample, mem_prompt, cache_cmp, cache_sel, cache_dsa, cache_idx, state_win, state_pool,
           cache_mem, page_table, norm_g, w_in, qk_g, w_cmp, pe_cmp, w_pool, pool_scale, mem_norm_g, w_mem_kv,
           w_br, w_out):
    past = page_table.shape[1] * PAGE_SIZE
    pos_p = jnp.arange(x_prompt.shape[1], dtype=jnp.int32)
    pos_s = past + jnp.arange(x_sample.shape[1], dtype=jnp.int32)
    xp, xs = x_prompt, x_sample
    st_p, st_s = [], []
    for l in range(DEPTH):
        w_in_b = prep_w_in(w_in[l])
        w_br_b = w_br[l].astype(BF16)
        w_out_b = w_out[l].astype(BF16)
        w_mem_kv_b = w_mem_kv[l].astype(BF16)
        xp, sp = prompt_layer(xp, mem_prompt, pos_p, norm_g[l], w_in_b, qk_g[l], w_cmp[l], pe_cmp[l],
                              w_pool[l].astype(BF16), pool_scale[l], mem_norm_g[l], w_mem_kv_b, w_br_b, w_out_b)
        xs, ss = sample_layer(xs, pos_s, l, page_table, cache_cmp, cache_sel, cache_dsa, cache_idx, state_win[l],
                              state_pool[l], cache_mem[l], norm_g[l], w_in_b, qk_g[l], w_cmp[l], pe_cmp[l],
                              w_pool[l], pool_scale[l], w_br_b, w_out_b)
        st_p.append(sp)
        st_s.append(ss)

    def stack(states, i):
        return jnp.stack([s[i] for s in states], axis=0)

    return (xp, xs,
            stack(st_p, 0), stack(st_s, 0),
            stack(st_p, 1), stack(st_s, 1),
            stack(st_p, 2), stack(st_s, 2),
            stack(st_p, 3), stack(st_s, 3),
            stack(st_p, 4), stack(st_s, 4),
            stack(st_p, 5), stack(st_s, 5),
            stack(st_p, 6))
```

```python
import functools

import numpy as np
import jax
import jax.numpy as jnp
from jax import lax
from jax.experimental import pallas as pl
from jax.experimental.pallas import tpu as pltpu

D_MODEL = 4096
DEPTH = 2
PAGE_SIZE = 128
HD = 128
H_A = 16
G_A = 2
R_A = H_A // G_A
CMP_LEN = 32
CMP_STRIDE = 16
SEL_LEN = 64
SEL_CHUNKS = SEL_LEN // CMP_STRIDE
N_SEL = 16
WINDOW = 512
H_B = 16
H_I = 32
D_I = 128
DSA_TOPK = 256
C_GROUPS = 4
POOL_WINDOWS = (2, 4, 8, 16)
C_WIDTH = 2048
C_GW = C_WIDTH // C_GROUPS
POOL_BUF = max(POOL_WINDOWS) - 1
H_M = 4
N_BRANCH = 4
W_A = H_A * HD
W_B = H_B * HD
W_M = H_M * HD
W_BR = W_A + W_B + C_WIDTH + W_M
QBLK = 128
ROPE_THETA = 10000.0
EPS = 1e-6
SCALE = HD ** -0.5
NEG = -1e30
FORCE = 1e9
INT_MIN = -2 ** 31

LANES = 128
BF16 = jnp.bfloat16

SRC_NAMES = ('q_a', 'kv_c', 'kv_s', 'kv_w', 'g_a', 's_a', 'q_b', 'kv_b', 'q_i', 'k_i', 'w_i', 's_b',
             'u_c', 's_c', 'q_m', 's_m', 'g_m')
SRC_SIZES = (W_A, 2 * G_A * HD, 2 * G_A * HD, 2 * G_A * HD, 3 * H_A, W_A, W_B, 2 * HD, H_I * D_I, D_I, H_I, W_B,
             C_WIDTH, C_WIDTH, W_M, W_M, N_BRANCH * D_MODEL)
SRC_OFF = dict(zip(SRC_NAMES, np.concatenate([[0], np.cumsum(SRC_SIZES)[:-1]]).tolist()))
SEG_LEN = dict(zip(SRC_NAMES, SRC_SIZES))
DST_ORDER = ('q_a', 's_a', 'q_b', 's_b', 'u_c', 's_c', 'q_i', 'g_m', 'kv_c', 'kv_s', 'kv_w', 'q_m', 's_m', 'kv_b',
             'k_i', 'g_a', 'w_i')
SEG_PAD = {n: -(-SEG_LEN[n] // LANES) * LANES for n in DST_ORDER}
SEG_OFF = dict(zip(DST_ORDER, np.concatenate([[0], np.cumsum([SEG_PAD[n] for n in DST_ORDER])[:-1]]).tolist()))
TN_FRONT = 512
N_FRONT = -(-sum(SEG_PAD.values()) // TN_FRONT) * TN_FRONT
VMEM_LIMIT = 48 * 1024 * 1024

N_CLASS = 4
WIN_KEYS = WINDOW + QBLK
IDX_CHUNK = 512
TB_POST = 256


def _cparams(*sem):
    return pltpu.CompilerParams(dimension_semantics=sem, vmem_limit_bytes=VMEM_LIMIT)


def _blk(name, width):
    assert SEG_OFF[name] % width == 0
    return SEG_OFF[name] // width


def _rmsnorm_cast_body(x_ref, g_ref, o_ref):
    x = x_ref[...]
    y = x * lax.rsqrt(jnp.mean(x * x, axis=-1, keepdims=True) + EPS)
    o_ref[...] = (y * g_ref[...]).astype(o_ref.dtype)


def rmsnorm_cast(x, g, tm):
    m, d = x.shape
    return pl.pallas_call(
        _rmsnorm_cast_body,
        grid=(m // tm,),
        in_specs=[pl.BlockSpec((tm, d), lambda i: (i, 0)), pl.BlockSpec((1, d), lambda i: (0, 0))],
        out_specs=pl.BlockSpec((tm, d), lambda i: (i, 0)),
        out_shape=jax.ShapeDtypeStruct((m, d), BF16),
        compiler_params=_cparams("parallel"),
        name="rmsnorm_cast",
    )(x, g.reshape(1, d))


def _matmul_body(a_ref, w_ref, o_ref):
    o_ref[...] = jnp.dot(a_ref[...], w_ref[...], preferred_element_type=jnp.float32)


def matmul(a, w, tm, tn):
    m, k = a.shape
    n = w.shape[1]
    return pl.pallas_call(
        _matmul_body,
        grid=(m // tm, n // tn),
        in_specs=[pl.BlockSpec((tm, k), lambda i, j: (i, 0)), pl.BlockSpec((k, tn), lambda i, j: (0, j))],
        out_specs=pl.BlockSpec((tm, tn), lambda i, j: (i, j)),
        out_shape=jax.ShapeDtypeStruct((m, n), jnp.float32),
        compiler_params=_cparams("parallel", "parallel"),
        name="matmul",
    )(a, w)


def _row_tile(m, big=1024):
    return big if m % big == 0 else m


def _merge_body(aa, ab, ap, am, ga, gb, gp, gm, wa, wb, wp, wm, o_ref):
    acc = jax.nn.sigmoid(ga[...]) * jnp.dot(aa[...], wa[...], preferred_element_type=jnp.float32)
    acc += jax.nn.sigmoid(gb[...]) * jnp.dot(ab[...], wb[...], preferred_element_type=jnp.float32)
    acc += jax.nn.sigmoid(gp[...]) * jnp.dot(ap[...], wp[...], preferred_element_type=jnp.float32)
    acc += jax.nn.sigmoid(gm[...]) * jnp.dot(am[...], wm[...], preferred_element_type=jnp.float32)
    o_ref[...] = acc.astype(o_ref.dtype)


def merge_branches(acts, z, w_br_b):
    m = z.shape[0]
    tm, tn = _row_tile(m, 512), 512
    gate0 = _blk('g_m', tn)
    per = D_MODEL // tn
    widths = (W_A, W_B, C_WIDTH, W_M)
    rows = np.concatenate([[0], np.cumsum(widths)[:-1]]).tolist()
    act_specs = [pl.BlockSpec((tm, w), lambda i, j: (i, 0)) for w in widths]
    gate_specs = [pl.BlockSpec((tm, tn), lambda i, j, b=b: (i, gate0 + b * per + j)) for b in range(N_BRANCH)]
    w_specs = [pl.BlockSpec((w, tn), lambda i, j, r=r, w=w: (r // w, j)) for w, r in zip(widths, rows)]
    assert all(r % w == 0 for w, r in zip(widths, rows))
    return pl.pallas_call(
        _merge_body,
        grid=(m // tm, D_MODEL // tn),
        in_specs=act_specs + gate_specs + w_specs,
        out_specs=pl.BlockSpec((tm, tn), lambda i, j: (i, j)),
        out_shape=jax.ShapeDtypeStruct((m, D_MODEL), BF16),
        compiler_params=_cparams("parallel", "arbitrary"),
        name="merge_branches",
    )(*acts, z, z, z, z, w_br_b, w_br_b, w_br_b, w_br_b)


def _out_body(x_ref, y_ref, w_ref, o_ref):
    o_ref[...] = x_ref[...] + jnp.dot(y_ref[...], w_ref[...], preferred_element_type=jnp.float32)


def out_proj(x2, y_mid, w_out_b):
    m = x2.shape[0]
    tm, tn = _row_tile(m), 512
    return pl.pallas_call(
        _out_body,
        grid=(m // tm, D_MODEL // tn),
        in_specs=[pl.BlockSpec((tm, tn), lambda i, j: (i, j)), pl.BlockSpec((tm, D_MODEL), lambda i, j: (i, 0)),
                  pl.BlockSpec((D_MODEL, tn), lambda i, j: (0, j))],
        out_specs=pl.BlockSpec((tm, tn), lambda i, j: (i, j)),
        out_shape=jax.ShapeDtypeStruct((m, D_MODEL), jnp.float32),
        compiler_params=_cparams("parallel", "arbitrary"),
        name="out_proj",
    )(x2, y_mid, w_out_b)


def _dot_nt(a, b):
    return lax.dot_general(a, b, (((1,), (1,)), ((), ())), preferred_element_type=jnp.float32)


def _rms(x, g):
    return x * lax.rsqrt(jnp.mean(x * x, axis=-1, keepdims=True) + EPS) * g


def _rope(y, cos, sin):
    return y * cos + pltpu.roll(y, HD // 2, 1) * sin


def _silu(s):
    return s * jax.nn.sigmoid(s)


def _attend(q, k, v, bias):
    s = _dot_nt(q, k) * SCALE + bias
    e = jnp.exp(s - jnp.max(s, axis=1, keepdims=True))
    d = jnp.sum(e, axis=1, keepdims=True)
    return jnp.dot(e.astype(BF16), v, preferred_element_type=jnp.float32) * (1.0 / d)


def _kpost_body(zs_ref, zw_ref, zb_ref, zi_ref, cos_ref, sin_ref, g_ref,
                sel_o, win_o, dsa_o, idx_o, ks_o, vs_o, kw_o, vw_o, kb_o, vb_o, ki_o):
    cos = cos_ref[...]
    sin = sin_ref[...]
    for z_ref, st_o, k_o, v_o, gi in ((zs_ref, sel_o, ks_o, vs_o, 2), (zw_ref, win_o, kw_o, vw_o, 3)):
        for g in range(G_A):
            k = _rope(_rms(z_ref[:, g * HD:(g + 1) * HD], g_ref[gi:gi + 1, :]), cos, sin)
            v = z_ref[:, (G_A + g) * HD:(G_A + g + 1) * HD]
            st_o[:, g * HD:(g + 1) * HD] = k
            st_o[:, (G_A + g) * HD:(G_A + g + 1) * HD] = v
            k_o[g] = k.astype(BF16)
            v_o[g] = v.astype(BF16)
    k = _rope(_rms(zb_ref[:, :HD], g_ref[5:6, :]), cos, sin)
    v = zb_ref[:, HD:]
    dsa_o[:, :HD] = k
    dsa_o[:, HD:] = v
    kb_o[...] = k.astype(BF16)
    vb_o[...] = v.astype(BF16)
    ki = _rope(zi_ref[...], cos, sin)
    idx_o[...] = ki
    ki_o[...] = ki.astype(BF16)


def key_post(z, cos, sin, qk_g, B, S):
    tb = TB_POST
    nb = S // tb
    zmap = lambda name, w: pl.BlockSpec((tb, w), lambda b, j: (b * nb + j, _blk(name, w)))
    row = lambda w: pl.BlockSpec((None, tb, w), lambda b, j: (b, j, 0))
    grp = pl.BlockSpec((None, G_A, tb, HD), lambda b, j: (b, 0, j, 0))
    f32, sds = jnp.float32, jax.ShapeDtypeStruct
    return pl.pallas_call(
        _kpost_body,
        grid=(B, nb),
        in_specs=[zmap('kv_s', 512), zmap('kv_w', 512), zmap('kv_b', 256), zmap('k_i', 128),
                  pl.BlockSpec((tb, HD), lambda b, j: (j, 0)), pl.BlockSpec((tb, HD), lambda b, j: (j, 0)),
                  pl.BlockSpec((8, HD), lambda b, j: (0, 0))],
        out_specs=[row(512), row(512), row(256), row(128), grp, grp, grp, grp, row(HD), row(HD), row(D_I)],
        out_shape=[sds((B, S, 512), f32), sds((B, S, 512), f32), sds((B, S, 256), f32), sds((B, S, D_I), f32),
                   sds((B, G_A, S, HD), BF16), sds((B, G_A, S, HD), BF16), sds((B, G_A, S, HD), BF16),
                   sds((B, G_A, S, HD), BF16), sds((B, S, HD), BF16), sds((B, S, HD), BF16), sds((B, S, D_I), BF16)],
        compiler_params=_cparams("parallel", "parallel"),
        name="key_post",
    )(z, z, z, z, cos, sin, qk_g)


def _nsa_body(zq_ref, zs_ref, ga_ref, cos_ref, sin_ref, g_ref, kc_ref, vc_ref, ks_ref, vs_ref, kw_ref, vw_ref,
              a_ref, e_ref, o_ref, qr_s, acc_s, sel_s, bias_s, wbias_s, *, per):
    j = pl.program_id(1)
    q0 = pl.multiple_of(j * QBLK, QBLK)
    pos = q0 + lax.broadcasted_iota(jnp.int32, (QBLK, 1), 0)
    lane = lax.broadcasted_iota(jnp.int32, (1, LANES), 1)
    gates = jax.nn.sigmoid(ga_ref[...])
    cos = cos_ref[...]
    sin = sin_ref[...]
    n_cmp = kc_ref.shape[1] - 1
    n_blk = e_ref.shape[1] // SEL_LEN

    w0 = pl.multiple_of(jnp.maximum(q0 - WINDOW, 0), QBLK)
    dist = pos - (w0 + lax.broadcasted_iota(jnp.int32, (1, WIN_KEYS), 1))
    wbias_s[...] = jnp.where(dist >= 0, jnp.where(dist < WINDOW, 0.0, NEG), NEG)
    cmask = (lane < n_cmp) & (lane * CMP_STRIDE + (CMP_LEN - 1) <= pos)
    cur = pos // SEL_LEN
    forced = (lane == 0) | (lane == cur) | (lane == cur - 1)

    for g in range(G_A):
        kw = kw_ref[g, pl.ds(w0, WIN_KEYS), :]
        vw = vw_ref[g, pl.ds(w0, WIN_KEYS), :]
        pg = jnp.zeros((QBLK, LANES), jnp.float32)
        for r in range(R_A):
            h = g * R_A + r
            qn = _rms(zq_ref[:, h * HD:(h + 1) * HD], g_ref[0:1, :])
            qr = _rope(qn, cos, sin).astype(BF16)
            qr_s[h] = qr
            s = jnp.where(cmask, _dot_nt(qn.astype(BF16), kc_ref[g]) * SCALE, NEG)
            e = jnp.where(cmask, jnp.exp(s - jnp.max(s, axis=1, keepdims=True)), 0.0)
            p = e / jnp.maximum(jnp.sum(e, axis=1, keepdims=True), 1e-30)
            pg = pg + p
            o_c = jnp.dot(p.astype(BF16), vc_ref[g], preferred_element_type=jnp.float32)
            o_w = _attend(qr, kw, vw, wbias_s[...])
            acc_s[:, h * HD:(h + 1) * HD] = (gates[:, h:h + 1] * o_c
                                            + gates[:, 2 * H_A + h:2 * H_A + h + 1] * o_w)

        hi = pg.astype(BF16)
        r1 = pg - hi.astype(jnp.float32)
        mid = r1.astype(BF16)
        lo = (r1 - mid.astype(jnp.float32)).astype(BF16)
        a = a_ref[...]
        score = (jnp.dot(hi, a, preferred_element_type=jnp.float32)
                 + jnp.dot(mid, a, preferred_element_type=jnp.float32)
                 + jnp.dot(lo, a, preferred_element_type=jnp.float32))
        score = jnp.where(forced, FORCE, jnp.where(lane <= cur, score, NEG))
        score = jnp.where(lane < n_blk, score, -3e38)
        rank = jnp.zeros((QBLK, LANES), jnp.int32)
        for jp in range(n_blk):
            col = score[:, jp:jp + 1]
            rank = rank + jnp.where(col > score, 1, jnp.where(col == score, jnp.where(lane > jp, 1, 0), 0))
        sel_s[g] = jnp.where(rank < min(N_SEL, n_blk), jnp.where(score > 0.5 * NEG, 1.0, 0.0), 0.0)

    def selected(nk):
        kidx = lax.broadcasted_iota(jnp.int32, (1, nk), 1)
        for g in range(G_A):
            selk = jnp.dot(sel_s[g].astype(BF16), e_ref[:, :nk], preferred_element_type=jnp.float32)
            bias_s[:, :nk] = jnp.where(kidx <= pos, jnp.where(selk > 0.5, 0.0, NEG), NEG)
            for r in range(R_A):
                h = g * R_A + r
                o_s = _attend(qr_s[h], ks_ref[g, :nk, :], vs_ref[g, :nk, :], bias_s[:, :nk])
                o = acc_s[:, h * HD:(h + 1) * HD] + gates[:, H_A + h:H_A + h + 1] * o_s
                o_ref[:, h * HD:(h + 1) * HD] = (o * _silu(zs_ref[:, h * HD:(h + 1) * HD])).astype(o_ref.dtype)

    for c in range(N_CLASS):
        pl.when(j // per == c)(functools.partial(selected, (c + 1) * per * QBLK))


def _sel_matrices(n_cmp, n_blk):
    a = np.zeros((LANES, LANES), np.float32)
    for i in range(n_cmp):
        for c in (i, i + 1):
            if c // SEL_CHUNKS < n_blk:
                a[i, c // SEL_CHUNKS] += 1.0
    e = np.zeros((LANES, n_blk * SEL_LEN), np.float32)
    for j in range(n_blk):
        e[j, j * SEL_LEN:(j + 1) * SEL_LEN] = 1.0
    return jnp.asarray(a, BF16), jnp.asarray(e, BF16)


def nsa_prompt_attention(z, cos, sin, qk_g, kc, vc, ks, vs, kw, vw, B, S):
    nqb = S // QBLK
    assert nqb % N_CLASS == 0 and S >= WIN_KEYS and S // CMP_STRIDE - 1 < LANES
    a_mat, e_mat = _sel_matrices(S // CMP_STRIDE - 1, S // SEL_LEN)
    zmap = lambda name, w: pl.BlockSpec((QBLK, w), lambda b, j: (b * nqb + j, _blk(name, w)))
    tab = pl.BlockSpec((QBLK, HD), lambda b, j: (j, 0))
    per_b = lambda n: pl.BlockSpec((None, G_A, n, HD), lambda b, j: (b, 0, 0, 0))
    return pl.pallas_call(
        functools.partial(_nsa_body, per=nqb // N_CLASS),
        grid=(B, nqb),
        in_specs=[zmap('q_a', W_A), zmap('s_a', W_A), zmap('g_a', LANES), tab, tab,
                  pl.BlockSpec((8, HD), lambda b, j: (0, 0)),
                  per_b(LANES), per_b(LANES), per_b(S), per_b(S), per_b(S), per_b(S),
                  pl.BlockSpec((LANES, LANES), lambda b, j: (0, 0)), pl.BlockSpec((LANES, S), lambda b, j: (0, 0))],
        out_specs=pl.BlockSpec((QBLK, W_A), lambda b, j: (b * nqb + j, 0)),
        out_shape=jax.ShapeDtypeStruct((B * S, W_A), BF16),
        scratch_shapes=[pltpu.VMEM((H_A, QBLK, HD), BF16), pltpu.VMEM((QBLK, W_A), jnp.float32),
                        pltpu.VMEM((G_A, QBLK, LANES), jnp.float32), pltpu.VMEM((QBLK, S), jnp.float32),
                        pltpu.VMEM((QBLK, WIN_KEYS), jnp.float32)],
        compiler_params=_cparams("parallel", "arbitrary"),
        name="nsa_prompt",
    )(z, z, z, cos, sin, qk_g, kc, vc, ks, vs, kw, vw, a_mat, e_mat)


def _topk_bias(score_s, key_s, bias_s, n, k_top):
    rows = score_s.shape[0]
    kidx = lax.broadcasted_iota(jnp.int32, (1, n), 1)
    bits = pltpu.bitcast(score_s[:, :n], jnp.int32)
    key_s[:, :n] = jnp.where(bits >= 0, bits, bits ^ 0x7FFFFFFF)

    def value_step(it, ans):
        cand = ans | jnp.left_shift(jnp.int32(1), 31 - it)
        cnt = jnp.sum(jnp.where(key_s[:, :n] >= (cand ^ INT_MIN), 1, 0), axis=1, keepdims=True)
        return jnp.where(cnt >= k_top, cand, ans)

    thr = lax.fori_loop(0, 32, value_step, jnp.zeros((rows, 1), jnp.int32)) ^ INT_MIN
    need = k_top - jnp.sum(jnp.where(key_s[:, :n] > thr, 1, 0), axis=1, keepdims=True)
    idx_bits = max(1, (n - 1).bit_length())

    def index_step(it, ans):
        cand = ans | jnp.left_shift(jnp.int32(1), idx_bits - 1 - it)
        below = jnp.where(key_s[:, :n] == thr, jnp.where(kidx < cand, 1, 0), 0)
        return jnp.where(jnp.sum(below, axis=1, keepdims=True) < need, cand, ans)

    cut = lax.fori_loop(0, idx_bits, index_step, jnp.zeros((rows, 1), jnp.int32))
    key = key_s[:, :n]
    chosen = jnp.where(key > thr, 1, jnp.where(key == thr, jnp.where(kidx <= cut, 1, 0), 0))
    bias_s[:, :n] = jnp.where(chosen > 0, jnp.where(score_s[:, :n] > 0.5 * NEG, 0.0, NEG), NEG)


def _dsa_body(zqi_ref, wi_ref, zqb_ref, zs_ref, cos_ref, sin_ref, g_ref, ki_ref, kb_ref, vb_ref, o_ref,
              qi_s, score_s, key_s, bias_s, *, nk, qb0, k_top):
    pos = (pl.program_id(1) + qb0) * QBLK + lax.broadcasted_iota(jnp.int32, (QBLK, 1), 0)
    kidx = lax.broadcasted_iota(jnp.int32, (1, nk), 1)
    cos = cos_ref[...]
    sin = sin_ref[...]
    for h in range(H_I):
        qi_s[h] = _rope(zqi_ref[:, h * D_I:(h + 1) * D_I], cos, sin).astype(BF16)

    for c in range(nk // IDX_CHUNK):
        k = ki_ref[c * IDX_CHUNK:(c + 1) * IDX_CHUNK, :]
        acc = jnp.zeros((QBLK, IDX_CHUNK), jnp.float32)
        for h in range(H_I):
            acc = acc + wi_ref[:, h:h + 1] * jnp.maximum(_dot_nt(qi_s[h], k), 0.0)
        score_s[:, c * IDX_CHUNK:(c + 1) * IDX_CHUNK] = acc * ((D_I ** -0.5) * (H_I ** -0.5))
    score_s[...] = jnp.where(kidx <= pos, score_s[...], NEG)
    _topk_bias(score_s, key_s, bias_s, nk, k_top)

    for h in range(H_B):
        cols = slice(h * HD, (h + 1) * HD)
        q = _rope(_rms(zqb_ref[:, cols], g_ref[4:5, :]), cos, sin).astype(BF16)
        o = _attend(q, kb_ref[:nk, :], vb_ref[:nk, :], bias_s[...])
        o_ref[:, cols] = (o * _silu(zs_ref[:, cols])).astype(o_ref.dtype)


def dsa_prompt_attention(z, cos, sin, qk_g, ki, kb, vb, B, S):
    nqb = S // QBLK
    per = nqb // N_CLASS
    assert nqb % N_CLASS == 0 and (per * QBLK) % IDX_CHUNK == 0
    tab = lambda c: pl.BlockSpec((QBLK, HD), lambda b, j: (c * per + j, 0))
    per_b = lambda d: pl.BlockSpec((None, S, d), lambda b, j: (b, 0, 0))
    outs = []
    for c in range(N_CLASS):
        nk = (c + 1) * per * QBLK
        zmap = lambda name, w, c=c: pl.BlockSpec((QBLK, w), lambda b, j: (b * nqb + c * per + j, _blk(name, w)))
        outs.append(pl.pallas_call(
            functools.partial(_dsa_body, nk=nk, qb0=c * per, k_top=min(DSA_TOPK, S // 4)),
            grid=(B, per),
            in_specs=[zmap('q_i', H_I * D_I), zmap('w_i', LANES), zmap('q_b', W_B), zmap('s_b', W_B), tab(c), tab(c),
                      pl.BlockSpec((8, HD), lambda b, j: (0, 0)), per_b(D_I), per_b(HD), per_b(HD)],
            out_specs=pl.BlockSpec((None, QBLK, W_B), lambda b, j: (b, j, 0)),
            out_shape=jax.ShapeDtypeStruct((B, per * QBLK, W_B), BF16),
            scratch_shapes=[pltpu.VMEM((H_I, QBLK, D_I), BF16), pltpu.VMEM((QBLK, nk), jnp.float32),
                            pltpu.VMEM((QBLK, nk), jnp.int32), pltpu.VMEM((QBLK, nk), jnp.float32)],
            compiler_params=_cparams("parallel", "arbitrary"),
            name=f"dsa_prompt_{nk}",
        )(z, z, z, z, cos, sin, qk_g, ki, kb, vb))
    return jnp.concatenate(outs, axis=1).reshape(B * S, W_B)


def _topk_mask_body(s_ref, o_ref, key_s, *, k_top):
    _topk_bias(s_ref, key_s, o_ref, s_ref.shape[1], k_top)


def topk_bias(score, k_top):
    r, n = score.shape
    return pl.pallas_call(
        functools.partial(_topk_mask_body, k_top=k_top),
        out_shape=jax.ShapeDtypeStruct((r, n), jnp.float32),
        scratch_shapes=[pltpu.VMEM((r, n), jnp.int32)],
        compiler_params=pltpu.CompilerParams(vmem_limit_bytes=VMEM_LIMIT),
        name="topk_bias",
    )(score)


def _pool_body(u_ref, up_ref, s_ref, w_ref, sc_ref, o_ref, ext_s):
    j = pl.program_id(1)
    tb = u_ref.shape[0]
    halo = POOL_BUF + 1
    prev = up_ref[tb - halo:, :]
    ext_s[:halo, :] = jnp.where(j > 0, prev, 0.0)
    ext_s[halo:, :] = u_ref[...]
    pos = j * tb + lax.broadcasted_iota(jnp.int32, (tb, 1), 0)
    for gi, w in enumerate(POOL_WINDOWS):
        cols = slice(gi * C_GW, (gi + 1) * C_GW)
        u = u_ref[:, cols]
        tot = u
        for k in range(1, w):
            tot = tot + ext_s[halo - k:halo - k + tb, cols]
        cnt = jnp.minimum(pos + 1, w).astype(jnp.float32)
        d = tot / cnt - u
        y = jnp.dot(d.astype(BF16), w_ref[gi], preferred_element_type=jnp.float32) * sc_ref[:, cols]
        o_ref[:, cols] = (y * _silu(s_ref[:, cols])).astype(o_ref.dtype)


def pool_prompt(z, w_pool_b, pool_scale, B, S):
    tb = TB_POST
    nb = S // tb
    ublk = _blk('u_c', C_WIDTH)
    return pl.pallas_call(
        _pool_body,
        grid=(B, nb),
        in_specs=[pl.BlockSpec((tb, C_WIDTH), lambda b, j: (b * nb + j, ublk)),
                  pl.BlockSpec((tb, C_WIDTH), lambda b, j: (b * nb + jnp.maximum(j - 1, 0), ublk)),
                  pl.BlockSpec((tb, C_WIDTH), lambda b, j: (b * nb + j, _blk('s_c', C_WIDTH))),
                  pl.BlockSpec((C_GROUPS, C_GW, C_GW), lambda b, j: (0, 0, 0)),
                  pl.BlockSpec((1, C_WIDTH), lambda b, j: (0, 0))],
        out_specs=pl.BlockSpec((tb, C_WIDTH), lambda b, j: (b * nb + j, 0)),
        out_shape=jax.ShapeDtypeStruct((B * S, C_WIDTH), BF16),
        scratch_shapes=[pltpu.VMEM((tb + POOL_BUF + 1, C_WIDTH), jnp.float32)],
        compiler_params=_cparams("parallel", "arbitrary"),
        name="pool_prompt",
    )(z, z, z, w_pool_b, pool_scale.reshape(1, C_WIDTH))


def _mem_body(zq_ref, zs_ref, g_ref, k_ref, v_ref, o_ref):
    zero = jnp.zeros((1, k_ref.shape[1]), jnp.float32)
    for h in range(H_M):
        cols = slice(h * HD, (h + 1) * HD)
        q = _rms(zq_ref[:, cols], g_ref[6:7, :]).astype(BF16)
        o = _attend(q, k_ref[h], v_ref[h], zero)
        o_ref[:, cols] = (o * _silu(zs_ref[:, cols])).astype(o_ref.dtype)


def mem_prompt_attention(z, qk_g, km, vm, B, S):
    tb = 2 * TB_POST
    nb = S // tb
    n_mem = km.shape[2]
    return pl.pallas_call(
        _mem_body,
        grid=(B, nb),
        in_specs=[pl.BlockSpec((tb, W_M), lambda b, j: (b * nb + j, _blk('q_m', W_M))),
                  pl.BlockSpec((tb, W_M), lambda b, j: (b * nb + j, _blk('s_m', W_M))),
                  pl.BlockSpec((8, HD), lambda b, j: (0, 0)),
                  pl.BlockSpec((None, H_M, n_mem, HD), lambda b, j: (b, 0, 0, 0)),
                  pl.BlockSpec((None, H_M, n_mem, HD), lambda b, j: (b, 0, 0, 0))],
        out_specs=pl.BlockSpec((tb, W_M), lambda b, j: (b * nb + j, 0)),
        out_shape=jax.ShapeDtypeStruct((B * S, W_M), BF16),
        compiler_params=_cparams("parallel", "parallel"),
        name="mem_prompt",
    )(z, z, qk_g, km, vm)


def prep_w_in(w):
    parts = []
    for name in DST_ORDER:
        p = w[:, SRC_OFF[name]:SRC_OFF[name] + SEG_LEN[name]].astype(BF16)
        if SEG_PAD[name] > SEG_LEN[name]:
            p = jnp.pad(p, ((0, 0), (0, SEG_PAD[name] - SEG_LEN[name])))
        parts.append(p)
    tail = N_FRONT - sum(SEG_PAD.values())
    if tail:
        parts.append(jnp.zeros((w.shape[0], tail), BF16))
    return jnp.concatenate(parts, axis=1)


def seg(z, name):
    return z[..., SEG_OFF[name]:SEG_OFF[name] + SEG_LEN[name]]


def rope_tables(pos):
    half = HD // 2
    inv_freq = ROPE_THETA ** (-jnp.arange(half, dtype=jnp.float32) / half)
    ang = pos.astype(jnp.float32)[:, None] * inv_freq[None, :]
    c, s = jnp.cos(ang), jnp.sin(ang)
    return jnp.concatenate([c, c], axis=-1), jnp.concatenate([-s, s], axis=-1)


def project(x2, norm_g, w_in_b):
    m = x2.shape[0]
    h = rmsnorm_cast(x2, norm_g, _row_tile(m, 512))
    return matmul(h, w_in_b, _row_tile(m), TN_FRONT)


def rmsnorm(x, g):
    y = x * lax.rsqrt(jnp.mean(x * x, axis=-1, keepdims=True) + EPS)
    return y * g


def rope(x, pos):
    half = x.shape[-1] // 2
    inv_freq = ROPE_THETA ** (-jnp.arange(half, dtype=jnp.float32) / half)
    ang = pos.astype(jnp.float32)[:, None] * inv_freq[None, :]
    cos = jnp.cos(ang)[:, None, :]
    sin = jnp.sin(ang)[:, None, :]
    x1, x2 = x[..., :half], x[..., half:]
    return jnp.concatenate([x1 * cos - x2 * sin, x2 * cos + x1 * sin], axis=-1)


def masked_softmax(s, mask):
    s = jnp.where(mask, s, NEG)
    return jnp.where(mask, jax.nn.softmax(s, axis=-1), 0.0)


def front_sample(z, pos, qk_g):
    B, T, _ = z.shape
    q_n = rmsnorm(seg(z, 'q_a').reshape(B, T, H_A, HD), qk_g[0])
    kv_s = seg(z, 'kv_s').reshape(B, T, 2, G_A, HD)
    kv_w = seg(z, 'kv_w').reshape(B, T, 2, G_A, HD)
    kv_b = seg(z, 'kv_b').reshape(B, T, 2, HD)
    k_b = rope(rmsnorm(kv_b[:, :, 0], qk_g[5])[:, :, None], pos)[:, :, 0]
    return {
        'q_n': q_n,
        'q_r': rope(q_n, pos),
        'kv_cmp': seg(z, 'kv_c').reshape(B, T, 2, G_A, HD),
        'kv_sel': jnp.stack([rope(rmsnorm(kv_s[:, :, 0], qk_g[2]), pos), kv_s[:, :, 1]], axis=2),
        'kv_win': jnp.stack([rope(rmsnorm(kv_w[:, :, 0], qk_g[3]), pos), kv_w[:, :, 1]], axis=2),
        'g_a': seg(z, 'g_a').reshape(B, T, 3, H_A),
        'q_b': rope(rmsnorm(seg(z, 'q_b').reshape(B, T, H_B, HD), qk_g[4]), pos),
        'kv_b': jnp.stack([k_b, kv_b[:, :, 1]], axis=2),
        'q_i': rope(seg(z, 'q_i').reshape(B, T, H_I, D_I), pos),
        'k_i': rope(seg(z, 'k_i')[:, :, None], pos)[:, :, 0],
        'w_i': seg(z, 'w_i'),
        'u_c': seg(z, 'u_c'),
        'q_m': rmsnorm(seg(z, 'q_m').reshape(B, T, H_M, HD), qk_g[6]),
        'silu': (seg(z, 's_a'), seg(z, 's_b'), seg(z, 's_c'), seg(z, 's_m')),
    }


def compress(rows, w_cmp, pe_cmp, g_k):
    B, L = rows.shape[:2]
    n_ch = L // CMP_STRIDE
    ch = rows[:, :n_ch * CMP_STRIDE].reshape(B, n_ch, CMP_STRIDE, 2, G_A, HD)
    pe = jnp.moveaxis(pe_cmp, 0, 1)
    c = (jnp.einsum('bnckgd,kgcde->kbnge', ch[:, :-1] + pe[:CMP_STRIDE], w_cmp[:, :, :CMP_STRIDE])
         + jnp.einsum('bnckgd,kgcde->kbnge', ch[:, 1:] + pe[CMP_STRIDE:], w_cmp[:, :, CMP_STRIDE:]))
    return rmsnorm(c[0], g_k), c[1]


def cmp_attend(q_n, pos_q, kc, vc):
    B, T = q_n.shape[:2]
    qg = q_n.reshape(B, T, G_A, R_A, HD)
    s = jnp.einsum('btgrd,bngd->btgrn', qg, kc) * SCALE
    blk_end = jnp.arange(kc.shape[1], dtype=jnp.int32) * CMP_STRIDE + (CMP_LEN - 1)
    mask = blk_end[None, :] <= pos_q[:, None]
    p = masked_softmax(s, mask[None, :, None, None, :])
    o = jnp.einsum('btgrn,bngd->btgrd', p, vc)
    return o.reshape(B, T, H_A, HD), p


def select_blocks(p_cmp, pos_q, n_blk):
    pg = p_cmp.sum(axis=3)
    zero = jnp.zeros(pg.shape[:-1] + (1,), pg.dtype)
    chunk = jnp.concatenate([pg, zero], -1) + jnp.concatenate([zero, pg], -1)
    chunk = jnp.pad(chunk, ((0, 0), (0, 0), (0, 0), (0, n_blk * SEL_CHUNKS - chunk.shape[-1])))
    score = chunk.reshape(chunk.shape[:-1] + (n_blk, SEL_CHUNKS)).sum(-1)
    j = jnp.arange(n_blk, dtype=jnp.int32)[None, :]
    cur = (pos_q // SEL_LEN)[:, None]
    forced = (j == 0) | (j == cur) | (j == cur - 1)
    valid = j <= cur
    score = jnp.where(forced[None, :, None], FORCE, jnp.where(valid[None, :, None], score, NEG))
    a, b = score[..., None, :], score[..., :, None]
    first = j[0][None, :] < j[0][:, None]
    rank = jnp.sum((a > b) | ((a == b) & first), axis=-1)
    return (rank < min(N_SEL, n_blk)) & (score > 0.5 * NEG)


def sel_attend_dense(q_r, pos_q, k, v, blk):
    B, T = q_r.shape[:2]
    L = k.shape[1]
    kpos = jnp.arange(L, dtype=jnp.int32)
    mask = jnp.repeat(blk, SEL_LEN, axis=-1)[..., :L] & (kpos[None, :] <= pos_q[:, None])[None, :, None, :]
    qg = q_r.reshape(B, T, G_A, R_A, HD)
    s = jnp.einsum('btgrd,bsgd->btgrs', qg, k) * SCALE
    p = masked_softmax(s, mask[:, :, :, None, :])
    o = jnp.einsum('btgrs,bsgd->btgrd', p, v)
    return o.reshape(B, T, H_A, HD)


def win_attend(q_r, pos_q, kw, vw, pos_k):
    B, T = q_r.shape[:2]
    qg = q_r.reshape(B, T, G_A, R_A, HD)
    s = jnp.einsum('btgrd,bsgd->btgrs', qg, kw) * SCALE
    d = pos_q[:, None] - pos_k[None, :]
    mask = (d >= 0) & (d < WINDOW) & (pos_k[None, :] >= 0)
    p = masked_softmax(s, mask[None, :, None, None, :])
    o = jnp.einsum('btgrs,bsgd->btgrd', p, vw)
    return o.reshape(B, T, H_A, HD)


def nsa_combine(g_a, o_c, o_s, o_w):
    g = jax.nn.sigmoid(g_a)[..., None]
    o = g[:, :, 0] * o_c + g[:, :, 1] * o_s + g[:, :, 2] * o_w
    return o.reshape(o.shape[0], o.shape[1], W_A)


def dsa_select_bias(q_i, w_i, k_i, pos_q, pos_k, k_top):
    B, T = q_i.shape[:2]
    L = k_i.shape[1]
    logits = jnp.einsum('bthd,bsd->bths', q_i, k_i) * (D_I ** -0.5)
    score = jnp.einsum('bths,bth->bts', jax.nn.relu(logits), w_i) * (H_I ** -0.5)
    score = jnp.where((pos_k[None, :] <= pos_q[:, None])[None], score, NEG)
    lp = -(-L // LANES) * LANES
    score = jnp.pad(score, ((0, 0), (0, 0), (0, lp - L)), constant_values=NEG)
    return topk_bias(score.reshape(B * T, lp), k_top).reshape(B, T, lp)[:, :, :L]


def dsa_attend_dense(q_b, k, v, bias):
    B, T = q_b.shape[:2]
    s = jnp.einsum('bthd,bsd->bths', q_b, k) * SCALE
    p = masked_softmax(s, (bias > 0.5 * NEG)[:, :, None, :])
    o = jnp.einsum('bths,bsd->bthd', p, v)
    return o.reshape(B, T, W_B)


def pool_mix(u_ext, pos_q, w_pool, pool_scale):
    B, n_ext, _ = u_ext.shape
    T = n_ext - POOL_BUF
    cs = jnp.concatenate([jnp.zeros((B, 1, C_WIDTH), jnp.float32), jnp.cumsum(u_ext, axis=1)], axis=1)
    end = cs[:, POOL_BUF + 1:]
    means = []
    for gi, w in enumerate(POOL_WINDOWS):
        ch = slice(gi * C_GW, (gi + 1) * C_GW)
        start = cs[:, POOL_BUF + 1 - w: POOL_BUF + 1 - w + T, ch]
        cnt = jnp.minimum(pos_q + 1, w).astype(jnp.float32)[None, :, None]
        means.append((end[:, :, ch] - start) / cnt)
    d = jnp.concatenate(means, axis=-1) - u_ext[:, POOL_BUF:]
    y = jnp.einsum('btgc,gce->btge', d.reshape(B, T, C_GROUPS, C_GW), w_pool)
    return y.reshape(B, T, C_WIDTH) * pool_scale


def mem_kv(mem, mem_norm_g, w_mem_kv_b, g_k):
    B, M, _ = mem.shape
    h = rmsnorm_cast(mem.reshape(B * M, D_MODEL), mem_norm_g, _row_tile(B * M, 512))
    kv = matmul(h, w_mem_kv_b, _row_tile(B * M), 512).reshape(B, M, 2, H_M, HD)
    return jnp.stack([rmsnorm(kv[:, :, 0], g_k), kv[:, :, 1]], axis=2)


def mem_attend(q_m, kv):
    B, T = q_m.shape[:2]
    s = jnp.einsum('bthd,bmhd->bthm', q_m, kv[:, :, 0]) * SCALE
    p = jax.nn.softmax(s, axis=-1)
    o = jnp.einsum('bthm,bmhd->bthd', p, kv[:, :, 1])
    return o.reshape(B, T, W_M)


PAGES_PER_STEP = 8


def _gather_body(pt_ref, *refs):
    o_ref = refs[-1]
    for r, p_ref in enumerate(refs[:-1]):
        o_ref[r * PAGE_SIZE:(r + 1) * PAGE_SIZE, :] = p_ref[...].astype(o_ref.dtype)


def gather_pages(pool, l, page_table, dtype):
    B, n_pages = page_table.shape
    tail = pool.shape[3:]
    w = int(np.prod(tail))
    pg = PAGES_PER_STEP
    assert n_pages % pg == 0
    flat = pool.reshape(pool.shape[:3] + (w,))
    page_spec = lambda r: pl.BlockSpec((None, None, PAGE_SIZE, w), lambda b, j, pt: (l, pt[b, j * pg + r], 0, 0))
    rows = pl.pallas_call(
        _gather_body,
        grid_spec=pltpu.PrefetchScalarGridSpec(
            num_scalar_prefetch=1, grid=(B, n_pages // pg),
            in_specs=[page_spec(r) for r in range(pg)],
            out_specs=pl.BlockSpec((None, pg * PAGE_SIZE, w), lambda b, j, pt: (b, j, 0))),
        out_shape=jax.ShapeDtypeStruct((B, n_pages * PAGE_SIZE, w), dtype),
        compiler_params=_cparams("parallel", "parallel"),
        name="gather_pages",
    )(page_table, *([flat] * pg))
    return rows.reshape((B, n_pages * PAGE_SIZE) + tail)


def prompt_layer(x, mem, pos, norm_g, w_in_b, qk_g, w_cmp, pe_cmp, w_pool_b, pool_scale, mem_norm_g, w_mem_kv_b,
                 w_br_b, w_out_b):
    B, S, _ = x.shape
    x2 = x.reshape(B * S, D_MODEL)
    z = project(x2, norm_g, w_in_b)
    cos, sin = rope_tables(pos)
    sel_st, win_st, dsa_st, idx_st, ks, vs, kw, vw, kb, vb, ki = key_post(z, cos, sin, qk_g, B, S)
    kv_cmp = seg(z, 'kv_c').reshape(B, S, 2, G_A, HD)
    kc, vc = compress(kv_cmp, w_cmp, pe_cmp, qk_g[1])
    pad_c = lambda t: jnp.pad(t, ((0, 0), (0, LANES - t.shape[1]), (0, 0), (0, 0))).transpose(0, 2, 1, 3).astype(BF16)
    a_a = nsa_prompt_attention(z, cos, sin, qk_g, pad_c(kc), pad_c(vc), ks, vs, kw, vw, B, S)
    a_b = dsa_prompt_attention(z, cos, sin, qk_g, ki, kb, vb, B, S)
    a_p = pool_prompt(z, w_pool_b, pool_scale, B, S)
    kv_m = mem_kv(mem, mem_norm_g, w_mem_kv_b, qk_g[7])
    kvm_t = kv_m.transpose(2, 0, 3, 1, 4).astype(BF16)
    a_m = mem_prompt_attention(z, qk_g, kvm_t[0], kvm_t[1], B, S)
    y_mid = merge_branches((a_a, a_b, a_p, a_m), z, w_br_b)
    y = out_proj(x2, y_mid, w_out_b).reshape(B, S, D_MODEL)
    wb = min(WINDOW, S)
    u_c = seg(z, 'u_c').reshape(B, S, C_WIDTH)
    return y, (kv_cmp, sel_st.reshape(B, S, 2, G_A, HD), dsa_st.reshape(B, S, 2, HD), idx_st,
               win_st[:, S - wb:].reshape(B, wb, 2, G_A, HD), u_c[:, S - POOL_BUF:], kv_m)


def sample_layer(x, pos, l, page_table, cache_cmp, cache_sel, cache_dsa, cache_idx, win_buf, pool_buf, mem_cache,
                 norm_g, w_in_b, qk_g, w_cmp, pe_cmp, w_pool, pool_scale, w_br_b, w_out_b):
    B, T, _ = x.shape
    past = page_table.shape[1] * PAGE_SIZE
    L = past + T
    x2 = x.reshape(B * T, D_MODEL)
    z2 = project(x2, norm_g, w_in_b)
    f = front_sample(z2.reshape(B, T, N_FRONT), pos, qk_g)
    cmp_rows = gather_pages(cache_cmp, l, page_table, jnp.float32)
    if (L // CMP_STRIDE) * CMP_STRIDE > past:
        cmp_rows = jnp.concatenate([cmp_rows, f['kv_cmp']], axis=1)
    kc, vc = compress(cmp_rows, w_cmp, pe_cmp, qk_g[1])
    o_c, p_c = cmp_attend(f['q_n'], pos, kc, vc)
    blk = select_blocks(p_c, pos, -(-L // SEL_LEN))
    sel_rows = jnp.concatenate([gather_pages(cache_sel, l, page_table, BF16), f['kv_sel'].astype(BF16)], axis=1)
    o_s = sel_attend_dense(f['q_r'], pos, sel_rows[:, :, 0], sel_rows[:, :, 1], blk)
    wb = win_buf.shape[1]
    win_rows = jnp.concatenate([win_buf, f['kv_win']], axis=1)
    pos_k = past - wb + jnp.arange(wb + T, dtype=jnp.int32)
    o_w = win_attend(f['q_r'], pos, win_rows[:, :, 0], win_rows[:, :, 1], pos_k)
    o_a = nsa_combine(f['g_a'], o_c, o_s, o_w)
    k_i_all = jnp.concatenate([gather_pages(cache_idx, l, page_table, BF16), f['k_i'].astype(BF16)], axis=1)
    bias_b = dsa_select_bias(f['q_i'], f['w_i'], k_i_all, pos, jnp.arange(L, dtype=jnp.int32),
                             min(DSA_TOPK, L // 4))
    kv_all = jnp.concatenate([gather_pages(cache_dsa, l, page_table, BF16), f['kv_b'].astype(BF16)], axis=1)
    o_b = dsa_attend_dense(f['q_b'], kv_all[:, :, 0], kv_all[:, :, 1], bias_b)
    u_ext = jnp.concatenate([pool_buf, f['u_c']], axis=1)
    o_p = pool_mix(u_ext, pos, w_pool, pool_scale)
    o_m = mem_attend(f['q_m'], mem_cache)
    acts = tuple((o * jax.nn.silu(s)).reshape(B * T, -1).astype(BF16)
                 for o, s in zip((o_a, o_b, o_p, o_m), f['silu']))
    y_mid = merge_branches(acts, z2, w_br_b)
    y = out_proj(x2, y_mid, w_out_b).reshape(B, T, D_MODEL)
    return y, (f['kv_cmp'], f['kv_sel'], f['kv_b'], f['k_i'], win_rows[:, T:], u_ext[:, T:])


def kernel(x_prompt, x_sample, mem_prompt, cache_cmp, cache_sel, cache_dsa, cache_idx, state_win, state_pool,
           cache_mem, page_table, norm_g, w_in, qk_g, w_cmp, pe_cmp, w_pool, pool_scale, mem_norm_g, w_mem_kv,
           w_br, w_out):
    past = page_table.shape[1] * PAGE_SIZE
    pos_p = jnp.arange(x_prompt.shape[1], dtype=jnp.int32)
    pos_s = past + jnp.arange(x_sample.shape[1], dtype=jnp.int32)
    xp, xs = x_prompt, x_sample
    st_p, st_s = [], []
    for l in range(DEPTH):
        w_in_b = prep_w_in(w_in[l])
        w_br_b = w_br[l].astype(BF16)
        w_out_b = w_out[l].astype(BF16)
        w_mem_kv_b = w_mem_kv[l].astype(BF16)
        xp, sp = prompt_layer(xp, mem_prompt, pos_p, norm_g[l], w_in_b, qk_g[l], w_cmp[l], pe_cmp[l],
                              w_pool[l].astype(BF16), pool_scale[l], mem_norm_g[l], w_mem_kv_b, w_br_b, w_out_b)
        xs, ss = sample_layer(xs, pos_s, l, page_table, cache_cmp, cache_sel, cache_dsa, cache_idx, state_win[l],
                              state_pool[l], cache_mem[l], norm_g[l], w_in_b, qk_g[l], w_cmp[l], pe_cmp[l],
                              w_pool[l], pool_scale[l], w_br_b, w_out_b)
        st_p.append(sp)
        st_s.append(ss)

    def stack(states, i):
        return jnp.stack([s[i] for s in states], axis=0)

    return (xp, xs,
            stack(st_p, 0), stack(st_s, 0),
            stack(st_p, 1), stack(st_s, 1),
            stack(st_p, 2), stack(st_s, 2),
            stack(st_p, 3), stack(st_s, 3),
            stack(st_p, 4), stack(st_s, 4),
            stack(st_p, 5), stack(st_s, 5),
            stack(st_p, 6))
```

```python
import functools

import numpy as np
import jax
import jax.numpy as jnp
from jax import lax
from jax.experimental import pallas as pl
from jax.experimental.pallas import tpu as pltpu

D_MODEL = 4096
DEPTH = 2
PAGE_SIZE = 128
HD = 128
H_A = 16
G_A = 2
R_A = H_A // G_A
CMP_LEN = 32
CMP_STRIDE = 16
SEL_LEN = 64
SEL_CHUNKS = SEL_LEN // CMP_STRIDE
N_SEL = 16
WINDOW = 512
H_B = 16
H_I = 32
D_I = 128
DSA_TOPK = 256
C_GROUPS = 4
POOL_WINDOWS = (2, 4, 8, 16)
C_WIDTH = 2048
C_GW = C_WIDTH // C_GROUPS
POOL_BUF = max(POOL_WINDOWS) - 1
H_M = 4
N_BRANCH = 4
W_A = H_A * HD
W_B = H_B * HD
W_M = H_M * HD
W_BR = W_A + W_B + C_WIDTH + W_M
QBLK = 128
ROPE_THETA = 10000.0
EPS = 1e-6
SCALE = HD ** -0.5
NEG = -1e30
FORCE = 1e9
INT_MIN = -2 ** 31

LANES = 128
BF16 = jnp.bfloat16

SRC_NAMES = ('q_a', 'kv_c', 'kv_s', 'kv_w', 'g_a', 's_a', 'q_b', 'kv_b', 'q_i', 'k_i', 'w_i', 's_b',
             'u_c', 's_c', 'q_m', 's_m', 'g_m')
SRC_SIZES = (W_A, 2 * G_A * HD, 2 * G_A * HD, 2 * G_A * HD, 3 * H_A, W_A, W_B, 2 * HD, H_I * D_I, D_I, H_I, W_B,
             C_WIDTH, C_WIDTH, W_M, W_M, N_BRANCH * D_MODEL)
SRC_OFF = dict(zip(SRC_NAMES, np.concatenate([[0], np.cumsum(SRC_SIZES)[:-1]]).tolist()))
SEG_LEN = dict(zip(SRC_NAMES, SRC_SIZES))
DST_ORDER = ('q_a', 's_a', 'q_b', 's_b', 'u_c', 's_c', 'q_i', 'g_m', 'kv_c', 'kv_s', 'kv_w', 'q_m', 's_m', 'kv_b',
             'k_i', 'g_a', 'w_i')
SEG_PAD = {n: -(-SEG_LEN[n] // LANES) * LANES for n in DST_ORDER}
SEG_OFF = dict(zip(DST_ORDER, np.concatenate([[0], np.cumsum([SEG_PAD[n] for n in DST_ORDER])[:-1]]).tolist()))
TN_FRONT = 512
N_FRONT = -(-sum(SEG_PAD.values()) // TN_FRONT) * TN_FRONT
VMEM_LIMIT = 48 * 1024 * 1024

N_CLASS = 4
WIN_KEYS = WINDOW + QBLK
IDX_CHUNK = 512
TB_POST = 256


def _cparams(*sem):
    return pltpu.CompilerParams(dimension_semantics=sem, vmem_limit_bytes=VMEM_LIMIT)


def _blk(name, width):
    assert SEG_OFF[name] % width == 0
    return SEG_OFF[name] // width


def _rmsnorm_cast_body(x_ref, g_ref, o_ref):
    x = x_ref[...]
    y = x * lax.rsqrt(jnp.mean(x * x, axis=-1, keepdims=True) + EPS)
    o_ref[...] = (y * g_ref[...]).astype(o_ref.dtype)


def rmsnorm_cast(x, g, tm):
    m, d = x.shape
    return pl.pallas_call(
        _rmsnorm_cast_body,
        grid=(m // tm,),
        in_specs=[pl.BlockSpec((tm, d), lambda i: (i, 0)), pl.BlockSpec((1, d), lambda i: (0, 0))],
        out_specs=pl.BlockSpec((tm, d), lambda i: (i, 0)),
        out_shape=jax.ShapeDtypeStruct((m, d), BF16),
        compiler_params=_cparams("parallel"),
        name="rmsnorm_cast",
    )(x, g.reshape(1, d))


def _matmul_body(a_ref, w_ref, o_ref):
    o_ref[...] = jnp.dot(a_ref[...], w_ref[...], preferred_element_type=jnp.float32)


def matmul(a, w, tm, tn):
    m, k = a.shape
    n = w.shape[1]
    return pl.pallas_call(
        _matmul_body,
        grid=(m // tm, n // tn),
        in_specs=[pl.BlockSpec((tm, k), lambda i, j: (i, 0)), pl.BlockSpec((k, tn), lambda i, j: (0, j))],
        out_specs=pl.BlockSpec((tm, tn), lambda i, j: (i, j)),
        out_shape=jax.ShapeDtypeStruct((m, n), jnp.float32),
        compiler_params=_cparams("parallel", "parallel"),
        name="matmul",
    )(a, w)


def _row_tile(m, big=1024):
    return big if m % big == 0 else m


def _merge_body(aa, ab, ap, am, ga, gb, gp, gm, wa, wb, wp, wm, o_ref):
    acc = jax.nn.sigmoid(ga[...]) * jnp.dot(aa[...], wa[...], preferred_element_type=jnp.float32)
    acc += jax.nn.sigmoid(gb[...]) * jnp.dot(ab[...], wb[...], preferred_element_type=jnp.float32)
    acc += jax.nn.sigmoid(gp[...]) * jnp.dot(ap[...], wp[...], preferred_element_type=jnp.float32)
    acc += jax.nn.sigmoid(gm[...]) * jnp.dot(am[...], wm[...], preferred_element_type=jnp.float32)
    o_ref[...] = acc.astype(o_ref.dtype)


def merge_branches(acts, z, w_br_b):
    m = z.shape[0]
    tm, tn = _row_tile(m, 512), 512
    gate0 = _blk('g_m', tn)
    per = D_MODEL // tn
    widths = (W_A, W_B, C_WIDTH, W_M)
    rows = np.concatenate([[0], np.cumsum(widths)[:-1]]).tolist()
    act_specs = [pl.BlockSpec((tm, w), lambda i, j: (i, 0)) for w in widths]
    gate_specs = [pl.BlockSpec((tm, tn), lambda i, j, b=b: (i, gate0 + b * per + j)) for b in range(N_BRANCH)]
    w_specs = [pl.BlockSpec((w, tn), lambda i, j, r=r, w=w: (r // w, j)) for w, r in zip(widths, rows)]
    assert all(r % w == 0 for w, r in zip(widths, rows))
    return pl.pallas_call(
        _merge_body,
        grid=(m // tm, D_MODEL // tn),
        in_specs=act_specs + gate_specs + w_specs,
        out_specs=pl.BlockSpec((tm, tn), lambda i, j: (i, j)),
        out_shape=jax.ShapeDtypeStruct((m, D_MODEL), BF16),
        compiler_params=_cparams("parallel", "arbitrary"),
        name="merge_branches",
    )(*acts, z, z, z, z, w_br_b, w_br_b, w_br_b, w_br_b)


def _out_body(x_ref, y_ref, w_ref, o_ref):
    o_ref[...] = x_ref[...] + jnp.dot(y_ref[...], w_ref[...], preferred_element_type=jnp.float32)


def out_proj(x2, y_mid, w_out_b):
    m = x2.shape[0]
    tm, tn = _row_tile(m), 512
    return pl.pallas_call(
        _out_body,
        grid=(m // tm, D_MODEL // tn),
        in_specs=[pl.BlockSpec((tm, tn), lambda i, j: (i, j)), pl.BlockSpec((tm, D_MODEL), lambda i, j: (i, 0)),
                  pl.BlockSpec((D_MODEL, tn), lambda i, j: (0, j))],
        out_specs=pl.BlockSpec((tm, tn), lambda i, j: (i, j)),
        out_shape=jax.ShapeDtypeStruct((m, D_MODEL), jnp.float32),
        compiler_params=_cparams("parallel", "arbitrary"),
        name="out_proj",
    )(x2, y_mid, w_out_b)


def _dot_nt(a, b):
    return lax.dot_general(a, b, (((1,), (1,)), ((), ())), preferred_element_type=jnp.float32)


def _rms(x, g):
    return x * lax.rsqrt(jnp.mean(x * x, axis=-1, keepdims=True) + EPS) * g


def _rope(y, cos, sin):
    return y * cos + pltpu.roll(y, HD // 2, 1) * sin


def _silu(s):
    return s * jax.nn.sigmoid(s)


def _attend(q, k, v, bias):
    s = _dot_nt(q, k) * SCALE + bias
    e = jnp.exp(s - jnp.max(s, axis=1, keepdims=True))
    d = jnp.sum(e, axis=1, keepdims=True)
    return jnp.dot(e.astype(BF16), v, preferred_element_type=jnp.float32) * (1.0 / d)


def _kpost_body(zs_ref, zw_ref, zb_ref, zi_ref, cos_ref, sin_ref, g_ref,
                sel_o, win_o, dsa_o, idx_o, ks_o, vs_o, kw_o, vw_o, kb_o, vb_o, ki_o):
    cos = cos_ref[...]
    sin = sin_ref[...]
    for z_ref, st_o, k_o, v_o, gi in ((zs_ref, sel_o, ks_o, vs_o, 2), (zw_ref, win_o, kw_o, vw_o, 3)):
        for g in range(G_A):
            k = _rope(_rms(z_ref[:, g * HD:(g + 1) * HD], g_ref[gi:gi + 1, :]), cos, sin)
            v = z_ref[:, (G_A + g) * HD:(G_A + g + 1) * HD]
            st_o[:, g * HD:(g + 1) * HD] = k
            st_o[:, (G_A + g) * HD:(G_A + g + 1) * HD] = v
            k_o[g] = k.astype(BF16)
            v_o[g] = v.astype(BF16)
    k = _rope(_rms(zb_ref[:, :HD], g_ref[5:6, :]), cos, sin)
    v = zb_ref[:, HD:]
    dsa_o[:, :HD] = k
    dsa_o[:, HD:] = v
    kb_o[...] = k.astype(BF16)
    vb_o[...] = v.astype(BF16)
    ki = _rope(zi_ref[...], cos, sin)
    idx_o[...] = ki
    ki_o[...] = ki.astype(BF16)


def key_post(z, cos, sin, qk_g, B, S):
    tb = TB_POST
    nb = S // tb
    zmap = lambda name, w: pl.BlockSpec((tb, w), lambda b, j: (b * nb + j, _blk(name, w)))
    row = lambda w: pl.BlockSpec((None, tb, w), lambda b, j: (b, j, 0))
    grp = pl.BlockSpec((None, G_A, tb, HD), lambda b, j: (b, 0, j, 0))
    f32, sds = jnp.float32, jax.ShapeDtypeStruct
    return pl.pallas_call(
        _kpost_body,
        grid=(B, nb),
        in_specs=[zmap('kv_s', 512), zmap('kv_w', 512), zmap('kv_b', 256), zmap('k_i', 128),
                  pl.BlockSpec((tb, HD), lambda b, j: (j, 0)), pl.BlockSpec((tb, HD), lambda b, j: (j, 0)),
                  pl.BlockSpec((8, HD), lambda b, j: (0, 0))],
        out_specs=[row(512), row(512), row(256), row(128), grp, grp, grp, grp, row(HD), row(HD), row(D_I)],
        out_shape=[sds((B, S, 512), f32), sds((B, S, 512), f32), sds((B, S, 256), f32), sds((B, S, D_I), f32),
                   sds((B, G_A, S, HD), BF16), sds((B, G_A, S, HD), BF16), sds((B, G_A, S, HD), BF16),
                   sds((B, G_A, S, HD), BF16), sds((B, S, HD), BF16), sds((B, S, HD), BF16), sds((B, S, D_I), BF16)],
        compiler_params=_cparams("parallel", "parallel"),
        name="key_post",
    )(z, z, z, z, cos, sin, qk_g)


def _nsa_body(zq_ref, zs_ref, ga_ref, cos_ref, sin_ref, g_ref, kc_ref, vc_ref, ks_ref, vs_ref, kw_ref, vw_ref,
              a_ref, e_ref, o_ref, qr_s, acc_s, sel_s, bias_s, wbias_s, *, per):
    j = pl.program_id(1)
    q0 = pl.multiple_of(j * QBLK, QBLK)
    pos = q0 + lax.broadcasted_iota(jnp.int32, (QBLK, 1), 0)
    lane = lax.broadcasted_iota(jnp.int32, (1, LANES), 1)
    gates = jax.nn.sigmoid(ga_ref[...])
    cos = cos_ref[...]
    sin = sin_ref[...]
    n_cmp = kc_ref.shape[1] - 1
    n_blk = e_ref.shape[1] // SEL_LEN

    w0 = pl.multiple_of(jnp.maximum(q0 - WINDOW, 0), QBLK)
    dist = pos - (w0 + lax.broadcasted_iota(jnp.int32, (1, WIN_KEYS), 1))
    wbias_s[...] = jnp.where(dist >= 0, jnp.where(dist < WINDOW, 0.0, NEG), NEG)
    cmask = (lane < n_cmp) & (lane * CMP_STRIDE + (CMP_LEN - 1) <= pos)
    cur = pos // SEL_LEN
    forced = (lane == 0) | (lane == cur) | (lane == cur - 1)

    for g in range(G_A):
        kw = kw_ref[g, pl.ds(w0, WIN_KEYS), :]
        vw = vw_ref[g, pl.ds(w0, WIN_KEYS), :]
        pg = jnp.zeros((QBLK, LANES), jnp.float32)
        for r in range(R_A):
            h = g * R_A + r
            qn = _rms(zq_ref[:, h * HD:(h + 1) * HD], g_ref[0:1, :])
            qr = _rope(qn, cos, sin).astype(BF16)
            qr_s[h] = qr
            s = jnp.where(cmask, _dot_nt(qn.astype(BF16), kc_ref[g]) * SCALE, NEG)
            e = jnp.where(cmask, jnp.exp(s - jnp.max(s, axis=1, keepdims=True)), 0.0)
            p = e / jnp.maximum(jnp.sum(e, axis=1, keepdims=True), 1e-30)
            pg = pg + p
            o_c = jnp.dot(p.astype(BF16), vc_ref[g], preferred_element_type=jnp.float32)
            o_w = _attend(qr, kw, vw, wbias_s[...])
            acc_s[:, h * HD:(h + 1) * HD] = (gates[:, h:h + 1] * o_c
                                            + gates[:, 2 * H_A + h:2 * H_A + h + 1] * o_w)

        hi = pg.astype(BF16)
        r1 = pg - hi.astype(jnp.float32)
        mid = r1.astype(BF16)
        lo = (r1 - mid.astype(jnp.float32)).astype(BF16)
        a = a_ref[...]
        score = (jnp.dot(hi, a, preferred_element_type=jnp.float32)
                 + jnp.dot(mid, a, preferred_element_type=jnp.float32)
                 + jnp.dot(lo, a, preferred_element_type=jnp.float32))
        score = jnp.where(forced, FORCE, jnp.where(lane <= cur, score, NEG))
        score = jnp.where(lane < n_blk, score, -3e38)
        rank = jnp.zeros((QBLK, LANES), jnp.int32)
        for jp in range(n_blk):
            col = score[:, jp:jp + 1]
            rank = rank + jnp.where(col > score, 1, jnp.where(col == score, jnp.where(lane > jp, 1, 0), 0))
        sel_s[g] = jnp.where(rank < min(N_SEL, n_blk), jnp.where(score > 0.5 * NEG, 1.0, 0.0), 0.0)

    def selected(nk):
        kidx = lax.broadcasted_iota(jnp.int32, (1, nk), 1)
        for g in range(G_A):
            selk = jnp.dot(sel_s[g].astype(BF16), e_ref[:, :nk], preferred_element_type=jnp.float32)
            bias_s[:, :nk] = jnp.where(kidx <= pos, jnp.where(selk > 0.5, 0.0, NEG), NEG)
            for r in range(R_A):
                h = g * R_A + r
                o_s = _attend(qr_s[h], ks_ref[g, :nk, :], vs_ref[g, :nk, :], bias_s[:, :nk])
                o = acc_s[:, h * HD:(h + 1) * HD] + gates[:, H_A + h:H_A + h + 1] * o_s
                o_ref[:, h * HD:(h + 1) * HD] = (o * _silu(zs_ref[:, h * HD:(h + 1) * HD])).astype(o_ref.dtype)

    for c in range(N_CLASS):
        pl.when(j // per == c)(functools.partial(selected, (c + 1) * per * QBLK))


def _sel_matrices(n_cmp, n_blk):
    a = np.zeros((LANES, LANES), np.float32)
    for i in range(n_cmp):
        for c in (i, i + 1):
            if c // SEL_CHUNKS < n_blk:
                a[i, c // SEL_CHUNKS] += 1.0
    e = np.zeros((LANES, n_blk * SEL_LEN), np.float32)
    for j in range(n_blk):
        e[j, j * SEL_LEN:(j + 1) * SEL_LEN] = 1.0
    return jnp.asarray(a, BF16), jnp.asarray(e, BF16)


def nsa_prompt_attention(z, cos, sin, qk_g, kc, vc, ks, vs, kw, vw, B, S):
    nqb = S // QBLK
    assert nqb % N_CLASS == 0 and S >= WIN_KEYS and S // CMP_STRIDE - 1 < LANES
    a_mat, e_mat = _sel_matrices(S // CMP_STRIDE - 1, S // SEL_LEN)
    zmap = lambda name, w: pl.BlockSpec((QBLK, w), lambda b, j: (b * nqb + j, _blk(name, w)))
    tab = pl.BlockSpec((QBLK, HD), lambda b, j: (j, 0))
    per_b = lambda n: pl.BlockSpec((None, G_A, n, HD), lambda b, j: (b, 0, 0, 0))
    return pl.pallas_call(
        functools.partial(_nsa_body, per=nqb // N_CLASS),
        grid=(B, nqb),
        in_specs=[zmap('q_a', W_A), zmap('s_a', W_A), zmap('g_a', LANES), tab, tab,
                  pl.BlockSpec((8, HD), lambda b, j: (0, 0)),
                  per_b(LANES), per_b(LANES), per_b(S), per_b(S), per_b(S), per_b(S),
                  pl.BlockSpec((LANES, LANES), lambda b, j: (0, 0)), pl.BlockSpec((LANES, S), lambda b, j: (0, 0))],
        out_specs=pl.BlockSpec((QBLK, W_A), lambda b, j: (b * nqb + j, 0)),
        out_shape=jax.ShapeDtypeStruct((B * S, W_A), BF16),
        scratch_shapes=[pltpu.VMEM((H_A, QBLK, HD), BF16), pltpu.VMEM((QBLK, W_A), jnp.float32),
                        pltpu.VMEM((G_A, QBLK, LANES), jnp.float32), pltpu.VMEM((QBLK, S), jnp.float32),
                        pltpu.VMEM((QBLK, WIN_KEYS), jnp.float32)],
        compiler_params=_cparams("parallel", "arbitrary"),
        name="nsa_prompt",
    )(z, z, z, cos, sin, qk_g, kc, vc, ks, vs, kw, vw, a_mat, e_mat)


def _topk_bias(score_s, key_s, bias_s, n, k_top):
    rows = score_s.shape[0]
    kidx = lax.broadcasted_iota(jnp.int32, (1, n), 1)
    bits = pltpu.bitcast(score_s[:, :n], jnp.int32)
    key_s[:, :n] = jnp.where(bits >= 0, bits, bits ^ 0x7FFFFFFF)

    def value_step(it, ans):
        cand = ans | jnp.left_shift(jnp.int32(1), 31 - it)
        cnt = jnp.sum(jnp.where(key_s[:, :n] >= (cand ^ INT_MIN), 1, 0), axis=1, keepdims=True)
        return jnp.where(cnt >= k_top, cand, ans)

    thr = lax.fori_loop(0, 32, value_step, jnp.zeros((rows, 1), jnp.int32)) ^ INT_MIN
    need = k_top - jnp.sum(jnp.where(key_s[:, :n] > thr, 1, 0), axis=1, keepdims=True)
    idx_bits = max(1, (n - 1).bit_length())

    def index_step(it, ans):
        cand = ans | jnp.left_shift(jnp.int32(1), idx_bits - 1 - it)
        below = jnp.where(key_s[:, :n] == thr, jnp.where(kidx < cand, 1, 0), 0)
        return jnp.where(jnp.sum(below, axis=1, keepdims=True) < need, cand, ans)

    cut = lax.fori_loop(0, idx_bits, index_step, jnp.zeros((rows, 1), jnp.int32))
    key = key_s[:, :n]
    chosen = jnp.where(key > thr, 1, jnp.where(key == thr, jnp.where(kidx <= cut, 1, 0), 0))
    bias_s[:, :n] = jnp.where(chosen > 0, jnp.where(score_s[:, :n] > 0.5 * NEG, 0.0, NEG), NEG)


def _dsa_body(zqi_ref, wi_ref, zqb_ref, zs_ref, cos_ref, sin_ref, g_ref, ki_ref, kb_ref, vb_ref, o_ref,
              qi_s, score_s, key_s, bias_s, *, nk, qb0, k_top):
    pos = (pl.program_id(1) + qb0) * QBLK + lax.broadcasted_iota(jnp.int32, (QBLK, 1), 0)
    kidx = lax.broadcasted_iota(jnp.int32, (1, nk), 1)
    cos = cos_ref[...]
    sin = sin_ref[...]
    for h in range(H_I):
        qi_s[h] = _rope(zqi_ref[:, h * D_I:(h + 1) * D_I], cos, sin).astype(BF16)

    for c in range(nk // IDX_CHUNK):
        k = ki_ref[c * IDX_CHUNK:(c + 1) * IDX_CHUNK, :]
        acc = jnp.zeros((QBLK, IDX_CHUNK), jnp.float32)
        for h in range(H_I):
            acc = acc + wi_ref[:, h:h + 1] * jnp.maximum(_dot_nt(qi_s[h], k), 0.0)
        score_s[:, c * IDX_CHUNK:(c + 1) * IDX_CHUNK] = acc * ((D_I ** -0.5) * (H_I ** -0.5))
    score_s[...] = jnp.where(kidx <= pos, score_s[...], NEG)
    _topk_bias(score_s, key_s, bias_s, nk, k_top)

    for h in range(H_B):
        cols = slice(h * HD, (h + 1) * HD)
        q = _rope(_rms(zqb_ref[:, cols], g_ref[4:5, :]), cos, sin).astype(BF16)
        o = _attend(q, kb_ref[:nk, :], vb_ref[:nk, :], bias_s[...])
        o_ref[:, cols] = (o * _silu(zs_ref[:, cols])).astype(o_ref.dtype)


def dsa_prompt_attention(z, cos, sin, qk_g, ki, kb, vb, B, S):
    nqb = S // QBLK
    per = nqb // N_CLASS
    assert nqb % N_CLASS == 0 and (per * QBLK) % IDX_CHUNK == 0
    tab = lambda c: pl.BlockSpec((QBLK, HD), lambda b, j: (c * per + j, 0))
    per_b = lambda d: pl.BlockSpec((None, S, d), lambda b, j: (b, 0, 0))
    outs = []
    for c in range(N_CLASS):
        nk = (c + 1) * per * QBLK
        zmap = lambda name, w, c=c: pl.BlockSpec((QBLK, w), lambda b, j: (b * nqb + c * per + j, _blk(name, w)))
        outs.append(pl.pallas_call(
            functools.partial(_dsa_body, nk=nk, qb0=c * per, k_top=min(DSA_TOPK, S // 4)),
            grid=(B, per),
            in_specs=[zmap('q_i', H_I * D_I), zmap('w_i', LANES), zmap('q_b', W_B), zmap('s_b', W_B), tab(c), tab(c),
                      pl.BlockSpec((8, HD), lambda b, j: (0, 0)), per_b(D_I), per_b(HD), per_b(HD)],
            out_specs=pl.BlockSpec((None, QBLK, W_B), lambda b, j: (b, j, 0)),
            out_shape=jax.ShapeDtypeStruct((B, per * QBLK, W_B), BF16),
            scratch_shapes=[pltpu.VMEM((H_I, QBLK, D_I), BF16), pltpu.VMEM((QBLK, nk), jnp.float32),
                            pltpu.VMEM((QBLK, nk), jnp.int32), pltpu.VMEM((QBLK, nk), jnp.float32)],
            compiler_params=_cparams("parallel", "arbitrary"),
            name=f"dsa_prompt_{nk}",
        )(z, z, z, z, cos, sin, qk_g, ki, kb, vb))
    return jnp.concatenate(outs, axis=1).reshape(B * S, W_B)


def _topk_mask_body(s_ref, o_ref, key_s, *, k_top):
    _topk_bias(s_ref, key_s, o_ref, s_ref.shape[1], k_top)


def topk_bias(score, k_top):
    r, n = score.shape
    return pl.pallas_call(
        functools.partial(_topk_mask_body, k_top=k_top),
        out_shape=jax.ShapeDtypeStruct((r, n), jnp.float32),
        scratch_shapes=[pltpu.VMEM((r, n), jnp.int32)],
        compiler_params=pltpu.CompilerParams(vmem_limit_bytes=VMEM_LIMIT),
        name="topk_bias",
    )(score)


def _pool_body(u_ref, up_ref, s_ref, w_ref, sc_ref, o_ref, ext_s):
    j = pl.program_id(1)
    tb = u_ref.shape[0]
    halo = POOL_BUF + 1
    prev = up_ref[tb - halo:, :]
    ext_s[:halo, :] = jnp.where(j > 0, prev, 0.0)
    ext_s[halo:, :] = u_ref[...]
    pos = j * tb + lax.broadcasted_iota(jnp.int32, (tb, 1), 0)
    for gi, w in enumerate(POOL_WINDOWS):
        cols = slice(gi * C_GW, (gi + 1) * C_GW)
        u = u_ref[:, cols]
        tot = u
        for k in range(1, w):
            tot = tot + ext_s[halo - k:halo - k + tb, cols]
        cnt = jnp.minimum(pos + 1, w).astype(jnp.float32)
        d = tot / cnt - u
        y = jnp.dot(d.astype(BF16), w_ref[gi], preferred_element_type=jnp.float32) * sc_ref[:, cols]
        o_ref[:, cols] = (y * _silu(s_ref[:, cols])).astype(o_ref.dtype)


def pool_prompt(z, w_pool_b, pool_scale, B, S):
    tb = TB_POST
    nb = S // tb
    ublk = _blk('u_c', C_WIDTH)
    return pl.pallas_call(
        _pool_body,
        grid=(B, nb),
        in_specs=[pl.BlockSpec((tb, C_WIDTH), lambda b, j: (b * nb + j, ublk)),
                  pl.BlockSpec((tb, C_WIDTH), lambda b, j: (b * nb + jnp.maximum(j - 1, 0), ublk)),
                  pl.BlockSpec((tb, C_WIDTH), lambda b, j: (b * nb + j, _blk('s_c', C_WIDTH))),
                  pl.BlockSpec((C_GROUPS, C_GW, C_GW), lambda b, j: (0, 0, 0)),
                  pl.BlockSpec((1, C_WIDTH), lambda b, j: (0, 0))],
        out_specs=pl.BlockSpec((tb, C_WIDTH), lambda b, j: (b * nb + j, 0)),
        out_shape=jax.ShapeDtypeStruct((B * S, C_WIDTH), BF16),
        scratch_shapes=[pltpu.VMEM((tb + POOL_BUF + 1, C_WIDTH), jnp.float32)],
        compiler_params=_cparams("parallel", "arbitrary"),
        name="pool_prompt",
    )(z, z, z, w_pool_b, pool_scale.reshape(1, C_WIDTH))


def _mem_body(zq_ref, zs_ref, g_ref, k_ref, v_ref, o_ref):
    zero = jnp.zeros((1, k_ref.shape[1]), jnp.float32)
    for h in range(H_M):
        cols = slice(h * HD, (h + 1) * HD)
        q = _rms(zq_ref[:, cols], g_ref[6:7, :]).astype(BF16)
        o = _attend(q, k_ref[h], v_ref[h], zero)
        o_ref[:, cols] = (o * _silu(zs_ref[:, cols])).astype(o_ref.dtype)


def mem_prompt_attention(z, qk_g, km, vm, B, S):
    tb = 2 * TB_POST
    nb = S // tb
    n_mem = km.shape[2]
    return pl.pallas_call(
        _mem_body,
        grid=(B, nb),
        in_specs=[pl.BlockSpec((tb, W_M), lambda b, j: (b * nb + j, _blk('q_m', W_M))),
                  pl.BlockSpec((tb, W_M), lambda b, j: (b * nb + j, _blk('s_m', W_M))),
                  pl.BlockSpec((8, HD), lambda b, j: (0, 0)),
                  pl.BlockSpec((None, H_M, n_mem, HD), lambda b, j: (b, 0, 0, 0)),
                  pl.BlockSpec((None, H_M, n_mem, HD), lambda b, j: (b, 0, 0, 0))],
        out_specs=pl.BlockSpec((tb, W_M), lambda b, j: (b * nb + j, 0)),
        out_shape=jax.ShapeDtypeStruct((B * S, W_M), BF16),
        compiler_params=_cparams("parallel", "parallel"),
        name="mem_prompt",
    )(z, z, qk_g, km, vm)


def prep_w_in(w):
    parts = []
    for name in DST_ORDER:
        p = w[:, SRC_OFF[name]:SRC_OFF[name] + SEG_LEN[name]].astype(BF16)
        if SEG_PAD[name] > SEG_LEN[name]:
            p = jnp.pad(p, ((0, 0), (0, SEG_PAD[name] - SEG_LEN[name])))
        parts.append(p)
    tail = N_FRONT - sum(SEG_PAD.values())
    if tail:
        parts.append(jnp.zeros((w.shape[0], tail), BF16))
    return jnp.concatenate(parts, axis=1)


def seg(z, name):
    return z[..., SEG_OFF[name]:SEG_OFF[name] + SEG_LEN[name]]


def rope_tables(pos):
    half = HD // 2
    inv_freq = ROPE_THETA ** (-jnp.arange(half, dtype=jnp.float32) / half)
    ang = pos.astype(jnp.float32)[:, None] * inv_freq[None, :]
    c, s = jnp.cos(ang), jnp.sin(ang)
    return jnp.concatenate([c, c], axis=-1), jnp.concatenate([-s, s], axis=-1)


def project(x2, norm_g, w_in_b):
    m = x2.shape[0]
    h = rmsnorm_cast(x2, norm_g, _row_tile(m, 512))
    return matmul(h, w_in_b, _row_tile(m), TN_FRONT)


def rmsnorm(x, g):
    y = x * lax.rsqrt(jnp.mean(x * x, axis=-1, keepdims=True) + EPS)
    return y * g


def rope(x, pos):
    half = x.shape[-1] // 2
    inv_freq = ROPE_THETA ** (-jnp.arange(half, dtype=jnp.float32) / half)
    ang = pos.astype(jnp.float32)[:, None] * inv_freq[None, :]
    cos = jnp.cos(ang)[:, None, :]
    sin = jnp.sin(ang)[:, None, :]
    x1, x2 = x[..., :half], x[..., half:]
    return jnp.concatenate([x1 * cos - x2 * sin, x2 * cos + x1 * sin], axis=-1)


def masked_softmax(s, mask):
    s = jnp.where(mask, s, NEG)
    return jnp.where(mask, jax.nn.softmax(s, axis=-1), 0.0)


def front_sample(z, pos, qk_g):
    B, T, _ = z.shape
    q_n = rmsnorm(seg(z, 'q_a').reshape(B, T, H_A, HD), qk_g[0])
    kv_s = seg(z, 'kv_s').reshape(B, T, 2, G_A, HD)
    kv_w = seg(z, 'kv_w').reshape(B, T, 2, G_A, HD)
    kv_b = seg(z, 'kv_b').reshape(B, T, 2, HD)
    k_b = rope(rmsnorm(kv_b[:, :, 0], qk_g[5])[:, :, None], pos)[:, :, 0]
    return {
        'q_n': q_n,
        'q_r': rope(q_n, pos),
        'kv_cmp': seg(z, 'kv_c').reshape(B, T, 2, G_A, HD),
        'kv_sel': jnp.stack([rope(rmsnorm(kv_s[:, :, 0], qk_g[2]), pos), kv_s[:, :, 1]], axis=2),
        'kv_win': jnp.stack([rope(rmsnorm(kv_w[:, :, 0], qk_g[3]), pos), kv_w[:, :, 1]], axis=2),
        'g_a': seg(z, 'g_a').reshape(B, T, 3, H_A),
        'q_b': rope(rmsnorm(seg(z, 'q_b').reshape(B, T, H_B, HD), qk_g[4]), pos),
        'kv_b': jnp.stack([k_b, kv_b[:, :, 1]], axis=2),
        'q_i': rope(seg(z, 'q_i').reshape(B, T, H_I, D_I), pos),
        'k_i': rope(seg(z, 'k_i')[:, :, None], pos)[:, :, 0],
        'w_i': seg(z, 'w_i'),
        'u_c': seg(z, 'u_c'),
        'q_m': rmsnorm(seg(z, 'q_m').reshape(B, T, H_M, HD), qk_g[6]),
        'silu': (seg(z, 's_a'), seg(z, 's_b'), seg(z, 's_c'), seg(z, 's_m')),
    }


def compress(rows, w_cmp, pe_cmp, g_k):
    B, L = rows.shape[:2]
    n_ch = L // CMP_STRIDE
    ch = rows[:, :n_ch * CMP_STRIDE].reshape(B, n_ch, CMP_STRIDE, 2, G_A, HD)
    ch = jnp.moveaxis(ch, 3, 0)
    pe = pe_cmp[:, None, None]
    c = (jnp.einsum('kbncgd,kgcde->kbnge', ch[:, :, :-1] + pe[:, :, :, :CMP_STRIDE], w_cmp[:, :, :CMP_STRIDE])
         + jnp.einsum('kbncgd,kgcde->kbnge', ch[:, :, 1:] + pe[:, :, :, CMP_STRIDE:], w_cmp[:, :, CMP_STRIDE:]))
    return rmsnorm(c[0], g_k), c[1]


def cmp_attend(q_n, pos_q, kc, vc):
    B, T = q_n.shape[:2]
    qg = q_n.reshape(B, T, G_A, R_A, HD)
    s = jnp.einsum('btgrd,bngd->btgrn', qg, kc) * SCALE
    blk_end = jnp.arange(kc.shape[1], dtype=jnp.int32) * CMP_STRIDE + (CMP_LEN - 1)
    mask = blk_end[None, :] <= pos_q[:, None]
    p = masked_softmax(s, mask[None, :, None, None, :])
    o = jnp.einsum('btgrn,bngd->btgrd', p, vc)
    return o.reshape(B, T, H_A, HD), p


def select_blocks(p_cmp, pos_q, n_blk):
    pg = p_cmp.sum(axis=3)
    zero = jnp.zeros(pg.shape[:-1] + (1,), pg.dtype)
    chunk = jnp.concatenate([pg, zero], -1) + jnp.concatenate([zero, pg], -1)
    chunk = jnp.pad(chunk, ((0, 0), (0, 0), (0, 0), (0, n_blk * SEL_CHUNKS - chunk.shape[-1])))
    score = chunk.reshape(chunk.shape[:-1] + (n_blk, SEL_CHUNKS)).sum(-1)
    j = jnp.arange(n_blk, dtype=jnp.int32)[None, :]
    cur = (pos_q // SEL_LEN)[:, None]
    forced = (j == 0) | (j == cur) | (j == cur - 1)
    valid = j <= cur
    score = jnp.where(forced[None, :, None], FORCE, jnp.where(valid[None, :, None], score, NEG))
    a, b = score[..., None, :], score[..., :, None]
    first = j[0][None, :] < j[0][:, None]
    rank = jnp.sum((a > b) | ((a == b) & first), axis=-1)
    return (rank < min(N_SEL, n_blk)) & (score > 0.5 * NEG)


def sel_attend_dense(q_r, pos_q, k, v, blk):
    B, T = q_r.shape[:2]
    L = k.shape[1]
    kpos = jnp.arange(L, dtype=jnp.int32)
    mask = jnp.repeat(blk, SEL_LEN, axis=-1)[..., :L] & (kpos[None, :] <= pos_q[:, None])[None, :, None, :]
    qg = q_r.reshape(B, T, G_A, R_A, HD)
    s = jnp.einsum('btgrd,bsgd->btgrs', qg, k) * SCALE
    p = masked_softmax(s, mask[:, :, :, None, :])
    o = jnp.einsum('btgrs,bsgd->btgrd', p, v)
    return o.reshape(B, T, H_A, HD)


def win_attend(q_r, pos_q, kw, vw, pos_k):
    B, T = q_r.shape[:2]
    qg = q_r.reshape(B, T, G_A, R_A, HD)
    s = jnp.einsum('btgrd,bsgd->btgrs', qg, kw) * SCALE
    d = pos_q[:, None] - pos_k[None, :]
    mask = (d >= 0) & (d < WINDOW) & (pos_k[None, :] >= 0)
    p = masked_softmax(s, mask[None, :, None, None, :])
    o = jnp.einsum('btgrs,bsgd->btgrd', p, vw)
    return o.reshape(B, T, H_A, HD)


def nsa_combine(g_a, o_c, o_s, o_w):
    g = jax.nn.sigmoid(g_a)[..., None]
    o = g[:, :, 0] * o_c + g[:, :, 1] * o_s + g[:, :, 2] * o_w
    return o.reshape(o.shape[0], o.shape[1], W_A)


def dsa_select_bias(q_i, w_i, k_i, pos_q, pos_k, k_top):
    B, T = q_i.shape[:2]
    L = k_i.shape[1]
    logits = jnp.einsum('bthd,bsd->bths', q_i, k_i) * (D_I ** -0.5)
    score = jnp.einsum('bths,bth->bts', jax.nn.relu(logits), w_i) * (H_I ** -0.5)
    score = jnp.where((pos_k[None, :] <= pos_q[:, None])[None], score, NEG)
    lp = -(-L // LANES) * LANES
    score = jnp.pad(score, ((0, 0), (0, 0), (0, lp - L)), constant_values=NEG)
    return topk_bias(score.reshape(B * T, lp), k_top).reshape(B, T, lp)[:, :, :L]


def dsa_attend_dense(q_b, k, v, bias):
    B, T = q_b.shape[:2]
    s = jnp.einsum('bthd,bsd->bths', q_b, k) * SCALE
    p = masked_softmax(s, (bias > 0.5 * NEG)[:, :, None, :])
    o = jnp.einsum('bths,bsd->bthd', p, v)
    return o.reshape(B, T, W_B)


def pool_mix(u_ext, pos_q, w_pool, pool_scale):
    B, n_ext, _ = u_ext.shape
    T = n_ext - POOL_BUF
    cs = jnp.concatenate([jnp.zeros((B, 1, C_WIDTH), jnp.float32), jnp.cumsum(u_ext, axis=1)], axis=1)
    end = cs[:, POOL_BUF + 1:]
    means = []
    for gi, w in enumerate(POOL_WINDOWS):
        ch = slice(gi * C_GW, (gi + 1) * C_GW)
        start = cs[:, POOL_BUF + 1 - w: POOL_BUF + 1 - w + T, ch]
        cnt = jnp.minimum(pos_q + 1, w).astype(jnp.float32)[None, :, None]
        means.append((end[:, :, ch] - start) / cnt)
    d = jnp.concatenate(means, axis=-1) - u_ext[:, POOL_BUF:]
    y = jnp.einsum('btgc,gce->btge', d.reshape(B, T, C_GROUPS, C_GW), w_pool)
    return y.reshape(B, T, C_WIDTH) * pool_scale


def mem_kv(mem, mem_norm_g, w_mem_kv_b, g_k):
    B, M, _ = mem.shape
    h = rmsnorm_cast(mem.reshape(B * M, D_MODEL), mem_norm_g, _row_tile(B * M, 512))
    kv = matmul(h, w_mem_kv_b, _row_tile(B * M), 512).reshape(B, M, 2, H_M, HD)
    return jnp.stack([rmsnorm(kv[:, :, 0], g_k), kv[:, :, 1]], axis=2)


def mem_attend(q_m, kv):
    B, T = q_m.shape[:2]
    s = jnp.einsum('bthd,bmhd->bthm', q_m, kv[:, :, 0]) * SCALE
    p = jax.nn.softmax(s, axis=-1)
    o = jnp.einsum('bthm,bmhd->bthd', p, kv[:, :, 1])
    return o.reshape(B, T, W_M)


def gather_pages(pool, l, page_table):
    rows = pool[l, page_table]
    return rows.reshape((page_table.shape[0], -1) + rows.shape[3:])


def prompt_layer(x, mem, pos, norm_g, w_in_b, qk_g, w_cmp, pe_cmp, w_pool_b, pool_scale, mem_norm_g, w_mem_kv_b,
                 w_br_b, w_out_b):
    B, S, _ = x.shape
    x2 = x.reshape(B * S, D_MODEL)
    z = project(x2, norm_g, w_in_b)
    cos, sin = rope_tables(pos)
    sel_st, win_st, dsa_st, idx_st, ks, vs, kw, vw, kb, vb, ki = key_post(z, cos, sin, qk_g, B, S)
    kv_cmp = seg(z, 'kv_c').reshape(B, S, 2, G_A, HD)
    kc, vc = compress(kv_cmp, w_cmp, pe_cmp, qk_g[1])
    pad_c = lambda t: jnp.pad(t, ((0, 0), (0, LANES - t.shape[1]), (0, 0), (0, 0))).transpose(0, 2, 1, 3).astype(BF16)
    a_a = nsa_prompt_attention(z, cos, sin, qk_g, pad_c(kc), pad_c(vc), ks, vs, kw, vw, B, S)
    a_b = dsa_prompt_attention(z, cos, sin, qk_g, ki, kb, vb, B, S)
    a_p = pool_prompt(z, w_pool_b, pool_scale, B, S)
    kv_m = mem_kv(mem, mem_norm_g, w_mem_kv_b, qk_g[7])
    kvm_t = kv_m.transpose(2, 0, 3, 1, 4).astype(BF16)
    a_m = mem_prompt_attention(z, qk_g, kvm_t[0], kvm_t[1], B, S)
    y_mid = merge_branches((a_a, a_b, a_p, a_m), z, w_br_b)
    y = out_proj(x2, y_mid, w_out_b).reshape(B, S, D_MODEL)
    wb = min(WINDOW, S)
    u_c = seg(z, 'u_c').reshape(B, S, C_WIDTH)
    return y, (kv_cmp, sel_st.reshape(B, S, 2, G_A, HD), dsa_st.reshape(B, S, 2, HD), idx_st,
               win_st[:, S - wb:].reshape(B, wb, 2, G_A, HD), u_c[:, S - POOL_BUF:], kv_m)


def sample_layer(x, pos, l, page_table, cache_cmp, cache_sel, cache_dsa, cache_idx, win_buf, pool_buf, mem_cache,
                 norm_g, w_in_b, qk_g, w_cmp, pe_cmp, w_pool, pool_scale, w_br_b, w_out_b):
    B, T, _ = x.shape
    past = page_table.shape[1] * PAGE_SIZE
    L = past + T
    x2 = x.reshape(B * T, D_MODEL)
    z2 = project(x2, norm_g, w_in_b)
    f = front_sample(z2.reshape(B, T, N_FRONT), pos, qk_g)
    cmp_rows = gather_pages(cache_cmp, l, page_table)
    if (L // CMP_STRIDE) * CMP_STRIDE > past:
        cmp_rows = jnp.concatenate([cmp_rows, f['kv_cmp']], axis=1)
    kc, vc = compress(cmp_rows, w_cmp, pe_cmp, qk_g[1])
    o_c, p_c = cmp_attend(f['q_n'], pos, kc, vc)
    blk = select_blocks(p_c, pos, -(-L // SEL_LEN))
    sel_past = lax.optimization_barrier(gather_pages(cache_sel, l, page_table))
    sel_rows = jnp.concatenate([sel_past, f['kv_sel']], axis=1)
    o_s = sel_attend_dense(f['q_r'], pos, sel_rows[:, :, 0], sel_rows[:, :, 1], blk)
    wb = win_buf.shape[1]
    win_rows = jnp.concatenate([win_buf, f['kv_win']], axis=1)
    pos_k = past - wb + jnp.arange(wb + T, dtype=jnp.int32)
    o_w = win_attend(f['q_r'], pos, win_rows[:, :, 0], win_rows[:, :, 1], pos_k)
    o_a = nsa_combine(f['g_a'], o_c, o_s, o_w)
    k_i_all = jnp.concatenate([gather_pages(cache_idx, l, page_table), f['k_i']], axis=1)
    bias_b = dsa_select_bias(f['q_i'], f['w_i'], k_i_all, pos, jnp.arange(L, dtype=jnp.int32),
                             min(DSA_TOPK, L // 4))
    kv_all = jnp.concatenate([gather_pages(cache_dsa, l, page_table), f['kv_b']], axis=1)
    o_b = dsa_attend_dense(f['q_b'], kv_all[:, :, 0], kv_all[:, :, 1], bias_b)
    u_ext = jnp.concatenate([pool_buf, f['u_c']], axis=1)
    o_p = pool_mix(u_ext, pos, w_pool, pool_scale)
    o_m = mem_attend(f['q_m'], mem_cache)
    acts = tuple((o * jax.nn.silu(s)).reshape(B * T, -1).astype(BF16)
                 for o, s in zip((o_a, o_b, o_p, o_m), f['silu']))
    y_mid = merge_branches(acts, z2, w_br_b)
    y = out_proj(x2, y_mid, w_out_b).reshape(B, T, D_MODEL)
    return y, (f['kv_cmp'], f['kv_sel'], f['kv_b'], f['k_i'], win_rows[:, T:], u_ext[:, T:])


def kernel(x_prompt, x_sample, mem_prompt, cache_cmp, cache_sel, cache_dsa, cache_idx, state_win, state_pool,
           cache_mem, page_table, norm_g, w_in, qk_g, w_cmp, pe_cmp, w_pool, pool_scale, mem_norm_g, w_mem_kv,
           w_br, w_out):
    past = page_table.shape[1] * PAGE_SIZE
    pos_p = jnp.arange(x_prompt.shape[1], dtype=jnp.int32)
    pos_s = past + jnp.arange(x_sample.shape[1], dtype=jnp.int32)
    xp, xs = x_prompt, x_sample
    st_p, st_s = [], []
    for l in range(DEPTH):
        w_in_b = prep_w_in(w_in[l])
        w_br_b = w_br[l].astype(BF16)
        w_out_b = w_out[l].astype(BF16)
        w_mem_kv_b = w_mem_kv[l].astype(BF16)
        xp, sp = prompt_layer(xp, mem_prompt, pos_p, norm_g[l], w_in_b, qk_g[l], w_cmp[l], pe_cmp[l],
                              w_pool[l].astype(BF16), pool_scale[l], mem_norm_g[l], w_mem_kv_b, w_br_b, w_out_b)
        xs, ss = sample_layer(xs, pos_s, l, page_table, cache_cmp, cache_sel, cache_dsa, cache_idx, state_win[l],
                              state_pool[l], cache_mem[l], norm_g[l], w_in_b, qk_g[l], w_cmp[l], pe_cmp[l],
                              w_pool[l], pool_scale[l], w_br_b, w_out_b)
        st_p.append(sp)
        st_s.append(ss)

    def stack(states, i):
        return jnp.stack([s[i] for s in states], axis=0)

    return (xp, xs,
            stack(st_p, 0), stack(st_s, 0),
            stack(st_p, 1), stack(st_s, 1),
            stack(st_p, 2), stack(st_s, 2),
            stack(st_p, 3), stack(st_s, 3),
            stack(st_p, 4), stack(st_s, 4),
            stack(st_p, 5), stack(st_s, 5),
            stack(st_p, 6))
```

```python
import functools

import numpy as np
import jax
import jax.numpy as jnp
from jax import lax
from jax.experimental import pallas as pl
from jax.experimental.pallas import tpu as pltpu

D_MODEL = 4096
DEPTH = 2
PAGE_SIZE = 128
HD = 128
H_A = 16
G_A = 2
R_A = H_A // G_A
CMP_LEN = 32
CMP_STRIDE = 16
SEL_LEN = 64
SEL_CHUNKS = SEL_LEN // CMP_STRIDE
N_SEL = 16
WINDOW = 512
H_B = 16
H_I = 32
D_I = 128
DSA_TOPK = 256
C_GROUPS = 4
POOL_WINDOWS = (2, 4, 8, 16)
C_WIDTH = 2048
C_GW = C_WIDTH // C_GROUPS
POOL_BUF = max(POOL_WINDOWS) - 1
H_M = 4
N_BRANCH = 4
W_A = H_A * HD
W_B = H_B * HD
W_M = H_M * HD
QBLK = 128
ROPE_THETA = 10000.0
EPS = 1e-6
SCALE = HD ** -0.5
NEG = -1e30
FORCE = 1e9
INT_MIN = -2 ** 31

LANES = 128
BF16 = jnp.bfloat16

SRC_NAMES = ('q_a', 'kv_c', 'kv_s', 'kv_w', 'g_a', 's_a', 'q_b', 'kv_b', 'q_i', 'k_i', 'w_i', 's_b',
             'u_c', 's_c', 'q_m', 's_m', 'g_m')
SRC_SIZES = (W_A, 2 * G_A * HD, 2 * G_A * HD, 2 * G_A * HD, 3 * H_A, W_A, W_B, 2 * HD, H_I * D_I, D_I, H_I, W_B,
             C_WIDTH, C_WIDTH, W_M, W_M, N_BRANCH * D_MODEL)
SRC_OFF = dict(zip(SRC_NAMES, np.concatenate([[0], np.cumsum(SRC_SIZES)[:-1]]).tolist()))
SEG_LEN = dict(zip(SRC_NAMES, SRC_SIZES))
DST_ORDER = ('q_a', 's_a', 'q_b', 's_b', 'u_c', 's_c', 'q_i', 'g_m', 'kv_c', 'kv_s', 'kv_w', 'q_m', 's_m', 'kv_b',
             'k_i', 'g_a', 'w_i')
SEG_PAD = {n: -(-SEG_LEN[n] // LANES) * LANES for n in DST_ORDER}
SEG_OFF = dict(zip(DST_ORDER, np.concatenate([[0], np.cumsum([SEG_PAD[n] for n in DST_ORDER])[:-1]]).tolist()))
TN_FRONT = 512
N_FRONT = -(-sum(SEG_PAD.values()) // TN_FRONT) * TN_FRONT
VMEM_LIMIT = 48 * 1024 * 1024

N_CLASS = 4
WIN_KEYS = WINDOW + QBLK
IDX_CHUNK = 512
TB_POST = 256


def _cparams(*sem):
    return pltpu.CompilerParams(dimension_semantics=sem, vmem_limit_bytes=VMEM_LIMIT)


def _blk(name, width):
    assert SEG_OFF[name] % width == 0
    return SEG_OFF[name] // width


def _rmsnorm_cast_body(x_ref, g_ref, o_ref):
    x = x_ref[...]
    y = x * lax.rsqrt(jnp.mean(x * x, axis=-1, keepdims=True) + EPS)
    o_ref[...] = (y * g_ref[...]).astype(o_ref.dtype)


def rmsnorm_cast(x, g, tm):
    m, d = x.shape
    return pl.pallas_call(
        _rmsnorm_cast_body,
        grid=(m // tm,),
        in_specs=[pl.BlockSpec((tm, d), lambda i: (i, 0)), pl.BlockSpec((1, d), lambda i: (0, 0))],
        out_specs=pl.BlockSpec((tm, d), lambda i: (i, 0)),
        out_shape=jax.ShapeDtypeStruct((m, d), BF16),
        compiler_params=_cparams("parallel"),
        name="rmsnorm_cast",
    )(x, g.reshape(1, d))


def _matmul_body(a_ref, w_ref, o_ref):
    o_ref[...] = jnp.dot(a_ref[...], w_ref[...], preferred_element_type=jnp.float32)


def matmul(a, w, tm, tn):
    m, k = a.shape
    n = w.shape[1]
    return pl.pallas_call(
        _matmul_body,
        grid=(m // tm, n // tn),
        in_specs=[pl.BlockSpec((tm, k), lambda i, j: (i, 0)), pl.BlockSpec((k, tn), lambda i, j: (0, j))],
        out_specs=pl.BlockSpec((tm, tn), lambda i, j: (i, j)),
        out_shape=jax.ShapeDtypeStruct((m, n), jnp.float32),
        compiler_params=_cparams("parallel", "parallel"),
        name="matmul",
    )(a, w)


def _row_tile(m, big=1024):
    return big if m % big == 0 else m


def _merge_body(aa, ab, ap, am, ga, gb, gp, gm, wa, wb, wp, wm, o_ref):
    acc = jax.nn.sigmoid(ga[...]) * jnp.dot(aa[...], wa[...], preferred_element_type=jnp.float32)
    acc += jax.nn.sigmoid(gb[...]) * jnp.dot(ab[...], wb[...], preferred_element_type=jnp.float32)
    acc += jax.nn.sigmoid(gp[...]) * jnp.dot(ap[...], wp[...], preferred_element_type=jnp.float32)
    acc += jax.nn.sigmoid(gm[...]) * jnp.dot(am[...], wm[...], preferred_element_type=jnp.float32)
    o_ref[...] = acc.astype(o_ref.dtype)


def merge_branches(acts, z, w_br_b):
    m = z.shape[0]
    tm, tn = _row_tile(m, 512), 512
    gate0 = _blk('g_m', tn)
    per = D_MODEL // tn
    widths = (W_A, W_B, C_WIDTH, W_M)
    rows = np.concatenate([[0], np.cumsum(widths)[:-1]]).tolist()
    act_specs = [pl.BlockSpec((tm, w), lambda i, j: (i, 0)) for w in widths]
    gate_specs = [pl.BlockSpec((tm, tn), lambda i, j, b=b: (i, gate0 + b * per + j)) for b in range(N_BRANCH)]
    w_specs = [pl.BlockSpec((w, tn), lambda i, j, r=r, w=w: (r // w, j)) for w, r in zip(widths, rows)]
    assert all(r % w == 0 for w, r in zip(widths, rows))
    return pl.pallas_call(
        _merge_body,
        grid=(m // tm, D_MODEL // tn),
        in_specs=act_specs + gate_specs + w_specs,
        out_specs=pl.BlockSpec((tm, tn), lambda i, j: (i, j)),
        out_shape=jax.ShapeDtypeStruct((m, D_MODEL), BF16),
        compiler_params=_cparams("parallel", "arbitrary"),
        name="merge_branches",
    )(*acts, z, z, z, z, w_br_b, w_br_b, w_br_b, w_br_b)


def _out_body(x_ref, y_ref, w_ref, o_ref):
    o_ref[...] = x_ref[...] + jnp.dot(y_ref[...], w_ref[...], preferred_element_type=jnp.float32)


def out_proj(x2, y_mid, w_out_b):
    m = x2.shape[0]
    tm, tn = _row_tile(m), 512
    return pl.pallas_call(
        _out_body,
        grid=(m // tm, D_MODEL // tn),
        in_specs=[pl.BlockSpec((tm, tn), lambda i, j: (i, j)), pl.BlockSpec((tm, D_MODEL), lambda i, j: (i, 0)),
                  pl.BlockSpec((D_MODEL, tn), lambda i, j: (0, j))],
        out_specs=pl.BlockSpec((tm, tn), lambda i, j: (i, j)),
        out_shape=jax.ShapeDtypeStruct((m, D_MODEL), jnp.float32),
        compiler_params=_cparams("parallel", "arbitrary"),
        name="out_proj",
    )(x2, y_mid, w_out_b)


def _dot_nt(a, b):
    return lax.dot_general(a, b, (((1,), (1,)), ((), ())), preferred_element_type=jnp.float32)


def _rms(x, g):
    return x * lax.rsqrt(jnp.mean(x * x, axis=-1, keepdims=True) + EPS) * g


def _rope(y, cos, sin):
    return y * cos + pltpu.roll(y, HD // 2, 1) * sin


def _silu(s):
    return s * jax.nn.sigmoid(s)


def _attend(q, k, v, bias):
    s = _dot_nt(q, k) * SCALE + bias
    e = jnp.exp(s - jnp.max(s, axis=1, keepdims=True))
    d = jnp.sum(e, axis=1, keepdims=True)
    return jnp.dot(e.astype(BF16), v, preferred_element_type=jnp.float32) * (1.0 / d)


def _kpost_body(zs_ref, zw_ref, zb_ref, zi_ref, cos_ref, sin_ref, g_ref,
                sel_o, win_o, dsa_o, idx_o, ks_o, vs_o, kw_o, vw_o, kb_o, vb_o, ki_o):
    cos = cos_ref[...]
    sin = sin_ref[...]
    for z_ref, st_o, k_o, v_o, gi in ((zs_ref, sel_o, ks_o, vs_o, 2), (zw_ref, win_o, kw_o, vw_o, 3)):
        for g in range(G_A):
            k = _rope(_rms(z_ref[:, g * HD:(g + 1) * HD], g_ref[gi:gi + 1, :]), cos, sin)
            v = z_ref[:, (G_A + g) * HD:(G_A + g + 1) * HD]
            st_o[:, g * HD:(g + 1) * HD] = k
            st_o[:, (G_A + g) * HD:(G_A + g + 1) * HD] = v
            k_o[g] = k.astype(BF16)
            v_o[g] = v.astype(BF16)
    k = _rope(_rms(zb_ref[:, :HD], g_ref[5:6, :]), cos, sin)
    v = zb_ref[:, HD:]
    dsa_o[:, :HD] = k
    dsa_o[:, HD:] = v
    kb_o[...] = k.astype(BF16)
    vb_o[...] = v.astype(BF16)
    ki = _rope(zi_ref[...], cos, sin)
    idx_o[...] = ki
    ki_o[...] = ki.astype(BF16)


def key_post(z, cos, sin, qk_g, B, S):
    tb = TB_POST
    nb = S // tb
    zmap = lambda name, w: pl.BlockSpec((tb, w), lambda b, j: (b * nb + j, _blk(name, w)))
    row = lambda w: pl.BlockSpec((None, tb, w), lambda b, j: (b, j, 0))
    grp = pl.BlockSpec((None, G_A, tb, HD), lambda b, j: (b, 0, j, 0))
    f32, sds = jnp.float32, jax.ShapeDtypeStruct
    return pl.pallas_call(
        _kpost_body,
        grid=(B, nb),
        in_specs=[zmap('kv_s', 512), zmap('kv_w', 512), zmap('kv_b', 256), zmap('k_i', 128),
                  pl.BlockSpec((tb, HD), lambda b, j: (j, 0)), pl.BlockSpec((tb, HD), lambda b, j: (j, 0)),
                  pl.BlockSpec((8, HD), lambda b, j: (0, 0))],
        out_specs=[row(512), row(512), row(256), row(128), grp, grp, grp, grp, row(HD), row(HD), row(D_I)],
        out_shape=[sds((B, S, 512), f32), sds((B, S, 512), f32), sds((B, S, 256), f32), sds((B, S, D_I), f32),
                   sds((B, G_A, S, HD), BF16), sds((B, G_A, S, HD), BF16), sds((B, G_A, S, HD), BF16),
                   sds((B, G_A, S, HD), BF16), sds((B, S, HD), BF16), sds((B, S, HD), BF16), sds((B, S, D_I), BF16)],
        compiler_params=_cparams("parallel", "parallel"),
        name="key_post",
    )(z, z, z, z, cos, sin, qk_g)


def _nsa_body(zq_ref, zs_ref, ga_ref, cos_ref, sin_ref, g_ref, kc_ref, vc_ref, ks_ref, vs_ref, kw_ref, vw_ref,
              a_ref, e_ref, o_ref, qr_s, acc_s, sel_s, bias_s, wbias_s, *, per):
    j = pl.program_id(1)
    q0 = pl.multiple_of(j * QBLK, QBLK)
    pos = q0 + lax.broadcasted_iota(jnp.int32, (QBLK, 1), 0)
    lane = lax.broadcasted_iota(jnp.int32, (1, LANES), 1)
    gates = jax.nn.sigmoid(ga_ref[...])
    cos = cos_ref[...]
    sin = sin_ref[...]
    n_cmp = kc_ref.shape[1] - 1
    n_blk = e_ref.shape[1] // SEL_LEN

    w0 = pl.multiple_of(jnp.maximum(q0 - WINDOW, 0), QBLK)
    dist = pos - (w0 + lax.broadcasted_iota(jnp.int32, (1, WIN_KEYS), 1))
    wbias_s[...] = jnp.where(dist >= 0, jnp.where(dist < WINDOW, 0.0, NEG), NEG)
    cmask = (lane < n_cmp) & (lane * CMP_STRIDE + (CMP_LEN - 1) <= pos)
    cur = pos // SEL_LEN
    forced = (lane == 0) | (lane == cur) | (lane == cur - 1)

    for g in range(G_A):
        kw = kw_ref[g, pl.ds(w0, WIN_KEYS), :]
        vw = vw_ref[g, pl.ds(w0, WIN_KEYS), :]
        pg = jnp.zeros((QBLK, LANES), jnp.float32)
        for r in range(R_A):
            h = g * R_A + r
            qn = _rms(zq_ref[:, h * HD:(h + 1) * HD], g_ref[0:1, :])
            qr = _rope(qn, cos, sin).astype(BF16)
            qr_s[h] = qr
            s = jnp.where(cmask, _dot_nt(qn.astype(BF16), kc_ref[g]) * SCALE, NEG)
            e = jnp.where(cmask, jnp.exp(s - jnp.max(s, axis=1, keepdims=True)), 0.0)
            p = e / jnp.maximum(jnp.sum(e, axis=1, keepdims=True), 1e-30)
            pg = pg + p
            o_c = jnp.dot(p.astype(BF16), vc_ref[g], preferred_element_type=jnp.float32)
            o_w = _attend(qr, kw, vw, wbias_s[...])
            acc_s[:, h * HD:(h + 1) * HD] = (gates[:, h:h + 1] * o_c
                                            + gates[:, 2 * H_A + h:2 * H_A + h + 1] * o_w)

        hi = pg.astype(BF16)
        r1 = pg - hi.astype(jnp.float32)
        mid = r1.astype(BF16)
        lo = (r1 - mid.astype(jnp.float32)).astype(BF16)
        a = a_ref[...]
        score = (jnp.dot(hi, a, preferred_element_type=jnp.float32)
                 + jnp.dot(mid, a, preferred_element_type=jnp.float32)
                 + jnp.dot(lo, a, preferred_element_type=jnp.float32))
        score = jnp.where(forced, FORCE, jnp.where(lane <= cur, score, NEG))
        score = jnp.where(lane < n_blk, score, -3e38)
        rank = jnp.zeros((QBLK, LANES), jnp.int32)
        for jp in range(n_blk):
            col = score[:, jp:jp + 1]
            rank = rank + jnp.where(col > score, 1, jnp.where(col == score, jnp.where(lane > jp, 1, 0), 0))
        sel_s[g] = jnp.where(rank < min(N_SEL, n_blk), jnp.where(score > 0.5 * NEG, 1.0, 0.0), 0.0)

    def selected(nk):
        kidx = lax.broadcasted_iota(jnp.int32, (1, nk), 1)
        for g in range(G_A):
            selk = jnp.dot(sel_s[g].astype(BF16), e_ref[:, :nk], preferred_element_type=jnp.float32)
            bias_s[:, :nk] = jnp.where(kidx <= pos, jnp.where(selk > 0.5, 0.0, NEG), NEG)
            for r in range(R_A):
                h = g * R_A + r
                o_s = _attend(qr_s[h], ks_ref[g, :nk, :], vs_ref[g, :nk, :], bias_s[:, :nk])
                o = acc_s[:, h * HD:(h + 1) * HD] + gates[:, H_A + h:H_A + h + 1] * o_s
                o_ref[:, h * HD:(h + 1) * HD] = (o * _silu(zs_ref[:, h * HD:(h + 1) * HD])).astype(o_ref.dtype)

    for c in range(N_CLASS):
        pl.when(j // per == c)(functools.partial(selected, (c + 1) * per * QBLK))


def _sel_matrices(n_cmp, n_blk):
    a = np.zeros((LANES, LANES), np.float32)
    for i in range(n_cmp):
        for c in (i, i + 1):
            if c // SEL_CHUNKS < n_blk:
                a[i, c // SEL_CHUNKS] += 1.0
    e = np.zeros((LANES, n_blk * SEL_LEN), np.float32)
    for j in range(n_blk):
        e[j, j * SEL_LEN:(j + 1) * SEL_LEN] = 1.0
    return jnp.asarray(a, BF16), jnp.asarray(e, BF16)


def nsa_prompt_attention(z, cos, sin, qk_g, kc, vc, ks, vs, kw, vw, B, S):
    nqb = S // QBLK
    assert nqb % N_CLASS == 0 and S >= WIN_KEYS and S // CMP_STRIDE - 1 < LANES
    a_mat, e_mat = _sel_matrices(S // CMP_STRIDE - 1, S // SEL_LEN)
    zmap = lambda name, w: pl.BlockSpec((QBLK, w), lambda b, j: (b * nqb + j, _blk(name, w)))
    tab = pl.BlockSpec((QBLK, HD), lambda b, j: (j, 0))
    per_b = lambda n: pl.BlockSpec((None, G_A, n, HD), lambda b, j: (b, 0, 0, 0))
    return pl.pallas_call(
        functools.partial(_nsa_body, per=nqb // N_CLASS),
        grid=(B, nqb),
        in_specs=[zmap('q_a', W_A), zmap('s_a', W_A), zmap('g_a', LANES), tab, tab,
                  pl.BlockSpec((8, HD), lambda b, j: (0, 0)),
                  per_b(LANES), per_b(LANES), per_b(S), per_b(S), per_b(S), per_b(S),
                  pl.BlockSpec((LANES, LANES), lambda b, j: (0, 0)), pl.BlockSpec((LANES, S), lambda b, j: (0, 0))],
        out_specs=pl.BlockSpec((QBLK, W_A), lambda b, j: (b * nqb + j, 0)),
        out_shape=jax.ShapeDtypeStruct((B * S, W_A), BF16),
        scratch_shapes=[pltpu.VMEM((H_A, QBLK, HD), BF16), pltpu.VMEM((QBLK, W_A), jnp.float32),
                        pltpu.VMEM((G_A, QBLK, LANES), jnp.float32), pltpu.VMEM((QBLK, S), jnp.float32),
                        pltpu.VMEM((QBLK, WIN_KEYS), jnp.float32)],
        compiler_params=_cparams("parallel", "arbitrary"),
        name="nsa_prompt",
    )(z, z, z, cos, sin, qk_g, kc, vc, ks, vs, kw, vw, a_mat, e_mat)


def _topk_bias(score_s, key_s, bias_s, n, k_top):
    rows = score_s.shape[0]
    kidx = lax.broadcasted_iota(jnp.int32, (1, n), 1)
    bits = pltpu.bitcast(score_s[:, :n], jnp.int32)
    key_s[:, :n] = jnp.where(bits >= 0, bits, bits ^ 0x7FFFFFFF)

    def value_step(it, ans):
        cand = ans | jnp.left_shift(jnp.int32(1), 31 - it)
        cnt = jnp.sum(jnp.where(key_s[:, :n] >= (cand ^ INT_MIN), 1, 0), axis=1, keepdims=True)
        return jnp.where(cnt >= k_top, cand, ans)

    thr = lax.fori_loop(0, 32, value_step, jnp.zeros((rows, 1), jnp.int32)) ^ INT_MIN
    need = k_top - jnp.sum(jnp.where(key_s[:, :n] > thr, 1, 0), axis=1, keepdims=True)
    idx_bits = max(1, (n - 1).bit_length())

    def index_step(it, ans):
        cand = ans | jnp.left_shift(jnp.int32(1), idx_bits - 1 - it)
        below = jnp.where(key_s[:, :n] == thr, jnp.where(kidx < cand, 1, 0), 0)
        return jnp.where(jnp.sum(below, axis=1, keepdims=True) < need, cand, ans)

    cut = lax.fori_loop(0, idx_bits, index_step, jnp.zeros((rows, 1), jnp.int32))
    key = key_s[:, :n]
    chosen = jnp.where(key > thr, 1, jnp.where(key == thr, jnp.where(kidx <= cut, 1, 0), 0))
    bias_s[:, :n] = jnp.where(chosen > 0, jnp.where(score_s[:, :n] > 0.5 * NEG, 0.0, NEG), NEG)


def _dsa_body(zqi_ref, wi_ref, zqb_ref, zs_ref, cos_ref, sin_ref, g_ref, ki_ref, kb_ref, vb_ref, o_ref,
              qi_s, score_s, key_s, bias_s, *, nk, qb0, k_top):
    pos = (pl.program_id(1) + qb0) * QBLK + lax.broadcasted_iota(jnp.int32, (QBLK, 1), 0)
    kidx = lax.broadcasted_iota(jnp.int32, (1, nk), 1)
    cos = cos_ref[...]
    sin = sin_ref[...]
    for h in range(H_I):
        qi_s[h] = _rope(zqi_ref[:, h * D_I:(h + 1) * D_I], cos, sin).astype(BF16)

    for c in range(nk // IDX_CHUNK):
        k = ki_ref[c * IDX_CHUNK:(c + 1) * IDX_CHUNK, :]
        acc = jnp.zeros((QBLK, IDX_CHUNK), jnp.float32)
        for h in range(H_I):
            acc = acc + wi_ref[:, h:h + 1] * jnp.maximum(_dot_nt(qi_s[h], k), 0.0)
        score_s[:, c * IDX_CHUNK:(c + 1) * IDX_CHUNK] = acc * ((D_I ** -0.5) * (H_I ** -0.5))
    score_s[...] = jnp.where(kidx <= pos, score_s[...], NEG)
    _topk_bias(score_s, key_s, bias_s, nk, k_top)

    for h in range(H_B):
        cols = slice(h * HD, (h + 1) * HD)
        q = _rope(_rms(zqb_ref[:, cols], g_ref[4:5, :]), cos, sin).astype(BF16)
        o = _attend(q, kb_ref[:nk, :], vb_ref[:nk, :], bias_s[...])
        o_ref[:, cols] = (o * _silu(zs_ref[:, cols])).astype(o_ref.dtype)


def dsa_prompt_attention(z, cos, sin, qk_g, ki, kb, vb, B, S):
    nqb = S // QBLK
    per = nqb // N_CLASS
    assert nqb % N_CLASS == 0 and (per * QBLK) % IDX_CHUNK == 0
    tab = lambda c: pl.BlockSpec((QBLK, HD), lambda b, j: (c * per + j, 0))
    per_b = lambda d: pl.BlockSpec((None, S, d), lambda b, j: (b, 0, 0))
    outs = []
    for c in range(N_CLASS):
        nk = (c + 1) * per * QBLK
        zmap = lambda name, w, c=c: pl.BlockSpec((QBLK, w), lambda b, j: (b * nqb + c * per + j, _blk(name, w)))
        outs.append(pl.pallas_call(
            functools.partial(_dsa_body, nk=nk, qb0=c * per, k_top=min(DSA_TOPK, S // 4)),
            grid=(B, per),
            in_specs=[zmap('q_i', H_I * D_I), zmap('w_i', LANES), zmap('q_b', W_B), zmap('s_b', W_B), tab(c), tab(c),
                      pl.BlockSpec((8, HD), lambda b, j: (0, 0)), per_b(D_I), per_b(HD), per_b(HD)],
            out_specs=pl.BlockSpec((None, QBLK, W_B), lambda b, j: (b, j, 0)),
            out_shape=jax.ShapeDtypeStruct((B, per * QBLK, W_B), BF16),
            scratch_shapes=[pltpu.VMEM((H_I, QBLK, D_I), BF16), pltpu.VMEM((QBLK, nk), jnp.float32),
                            pltpu.VMEM((QBLK, nk), jnp.int32), pltpu.VMEM((QBLK, nk), jnp.float32)],
            compiler_params=_cparams("parallel", "arbitrary"),
            name=f"dsa_prompt_{nk}",
        )(z, z, z, z, cos, sin, qk_g, ki, kb, vb))
    return jnp.concatenate(outs, axis=1).reshape(B * S, W_B)


def _topk_mask_body(s_ref, o_ref, key_s, *, k_top):
    _topk_bias(s_ref, key_s, o_ref, s_ref.shape[1], k_top)


def topk_bias(score, k_top):
    r, n = score.shape
    return pl.pallas_call(
        functools.partial(_topk_mask_body, k_top=k_top),
        out_shape=jax.ShapeDtypeStruct((r, n), jnp.float32),
        scratch_shapes=[pltpu.VMEM((r, n), jnp.int32)],
        compiler_params=pltpu.CompilerParams(vmem_limit_bytes=VMEM_LIMIT),
        name="topk_bias",
    )(score)


def _pool_body(u_ref, up_ref, s_ref, w_ref, sc_ref, o_ref, ext_s):
    j = pl.program_id(1)
    tb = u_ref.shape[0]
    halo = POOL_BUF + 1
    prev = up_ref[tb - halo:, :]
    ext_s[:halo, :] = jnp.where(j > 0, prev, 0.0)
    ext_s[halo:, :] = u_ref[...]
    pos = j * tb + lax.broadcasted_iota(jnp.int32, (tb, 1), 0)
    for gi, w in enumerate(POOL_WINDOWS):
        cols = slice(gi * C_GW, (gi + 1) * C_GW)
        u = u_ref[:, cols]
        tot = u
        for k in range(1, w):
            tot = tot + ext_s[halo - k:halo - k + tb, cols]
        cnt = jnp.minimum(pos + 1, w).astype(jnp.float32)
        d = tot / cnt - u
        y = jnp.dot(d.astype(BF16), w_ref[gi], preferred_element_type=jnp.float32) * sc_ref[:, cols]
        o_ref[:, cols] = (y * _silu(s_ref[:, cols])).astype(o_ref.dtype)


def pool_prompt(z, w_pool_b, pool_scale, B, S):
    tb = TB_POST
    nb = S // tb
    ublk = _blk('u_c', C_WIDTH)
    return pl.pallas_call(
        _pool_body,
        grid=(B, nb),
        in_specs=[pl.BlockSpec((tb, C_WIDTH), lambda b, j: (b * nb + j, ublk)),
                  pl.BlockSpec((tb, C_WIDTH), lambda b, j: (b * nb + jnp.maximum(j - 1, 0), ublk)),
                  pl.BlockSpec((tb, C_WIDTH), lambda b, j: (b * nb + j, _blk('s_c', C_WIDTH))),
                  pl.BlockSpec((C_GROUPS, C_GW, C_GW), lambda b, j: (0, 0, 0)),
                  pl.BlockSpec((1, C_WIDTH), lambda b, j: (0, 0))],
        out_specs=pl.BlockSpec((tb, C_WIDTH), lambda b, j: (b * nb + j, 0)),
        out_shape=jax.ShapeDtypeStruct((B * S, C_WIDTH), BF16),
        scratch_shapes=[pltpu.VMEM((tb + POOL_BUF + 1, C_WIDTH), jnp.float32)],
        compiler_params=_cparams("parallel", "arbitrary"),
        name="pool_prompt",
    )(z, z, z, w_pool_b, pool_scale.reshape(1, C_WIDTH))


def _mem_body(zq_ref, zs_ref, g_ref, k_ref, v_ref, o_ref):
    zero = jnp.zeros((1, k_ref.shape[1]), jnp.float32)
    for h in range(H_M):
        cols = slice(h * HD, (h + 1) * HD)
        q = _rms(zq_ref[:, cols], g_ref[6:7, :]).astype(BF16)
        o = _attend(q, k_ref[h], v_ref[h], zero)
        o_ref[:, cols] = (o * _silu(zs_ref[:, cols])).astype(o_ref.dtype)


def mem_prompt_attention(z, qk_g, km, vm, B, S):
    tb = 2 * TB_POST
    nb = S // tb
    n_mem = km.shape[2]
    return pl.pallas_call(
        _mem_body,
        grid=(B, nb),
        in_specs=[pl.BlockSpec((tb, W_M), lambda b, j: (b * nb + j, _blk('q_m', W_M))),
                  pl.BlockSpec((tb, W_M), lambda b, j: (b * nb + j, _blk('s_m', W_M))),
                  pl.BlockSpec((8, HD), lambda b, j: (0, 0)),
                  pl.BlockSpec((None, H_M, n_mem, HD), lambda b, j: (b, 0, 0, 0)),
                  pl.BlockSpec((None, H_M, n_mem, HD), lambda b, j: (b, 0, 0, 0))],
        out_specs=pl.BlockSpec((tb, W_M), lambda b, j: (b * nb + j, 0)),
        out_shape=jax.ShapeDtypeStruct((B * S, W_M), BF16),
        compiler_params=_cparams("parallel", "parallel"),
        name="mem_prompt",
    )(z, z, qk_g, km, vm)


def prep_w_in(w):
    parts = []
    for name in DST_ORDER:
        p = w[:, SRC_OFF[name]:SRC_OFF[name] + SEG_LEN[name]].astype(BF16)
        if SEG_PAD[name] > SEG_LEN[name]:
            p = jnp.pad(p, ((0, 0), (0, SEG_PAD[name] - SEG_LEN[name])))
        parts.append(p)
    tail = N_FRONT - sum(SEG_PAD.values())
    if tail:
        parts.append(jnp.zeros((w.shape[0], tail), BF16))
    return jnp.concatenate(parts, axis=1)


def seg(z, name):
    return z[..., SEG_OFF[name]:SEG_OFF[name] + SEG_LEN[name]]


def rope_tables(pos):
    half = HD // 2
    inv_freq = ROPE_THETA ** (-jnp.arange(half, dtype=jnp.float32) / half)
    ang = pos.astype(jnp.float32)[:, None] * inv_freq[None, :]
    c, s = jnp.cos(ang), jnp.sin(ang)
    return jnp.concatenate([c, c], axis=-1), jnp.concatenate([-s, s], axis=-1)


def project(x2, norm_g, w_in_b):
    m = x2.shape[0]
    h = rmsnorm_cast(x2, norm_g, _row_tile(m, 512))
    return matmul(h, w_in_b, _row_tile(m), TN_FRONT)


def rmsnorm(x, g):
    y = x * lax.rsqrt(jnp.mean(x * x, axis=-1, keepdims=True) + EPS)
    return y * g


def rope(x, pos):
    half = x.shape[-1] // 2
    inv_freq = ROPE_THETA ** (-jnp.arange(half, dtype=jnp.float32) / half)
    ang = pos.astype(jnp.float32)[:, None] * inv_freq[None, :]
    cos = jnp.cos(ang)[:, None, :]
    sin = jnp.sin(ang)[:, None, :]
    x1, x2 = x[..., :half], x[..., half:]
    return jnp.concatenate([x1 * cos - x2 * sin, x2 * cos + x1 * sin], axis=-1)


def masked_softmax(s, mask):
    s = jnp.where(mask, s, NEG)
    return jnp.where(mask, jax.nn.softmax(s, axis=-1), 0.0)


def front_sample(z, pos, qk_g):
    B, T, _ = z.shape
    q_n = rmsnorm(seg(z, 'q_a').reshape(B, T, H_A, HD), qk_g[0])
    kv_s = seg(z, 'kv_s').reshape(B, T, 2, G_A, HD)
    kv_w = seg(z, 'kv_w').reshape(B, T, 2, G_A, HD)
    kv_b = seg(z, 'kv_b').reshape(B, T, 2, HD)
    k_b = rope(rmsnorm(kv_b[:, :, 0], qk_g[5])[:, :, None], pos)[:, :, 0]
    return {
        'q_n': q_n,
        'q_r': rope(q_n, pos),
        'kv_cmp': seg(z, 'kv_c').reshape(B, T, 2, G_A, HD),
        'kv_sel': jnp.stack([rope(rmsnorm(kv_s[:, :, 0], qk_g[2]), pos), kv_s[:, :, 1]], axis=2),
        'kv_win': jnp.stack([rope(rmsnorm(kv_w[:, :, 0], qk_g[3]), pos), kv_w[:, :, 1]], axis=2),
        'g_a': seg(z, 'g_a').reshape(B, T, 3, H_A),
        'q_b': rope(rmsnorm(seg(z, 'q_b').reshape(B, T, H_B, HD), qk_g[4]), pos),
        'kv_b': jnp.stack([k_b, kv_b[:, :, 1]], axis=2),
        'q_i': rope(seg(z, 'q_i').reshape(B, T, H_I, D_I), pos),
        'k_i': rope(seg(z, 'k_i')[:, :, None], pos)[:, :, 0],
        'w_i': seg(z, 'w_i'),
        'u_c': seg(z, 'u_c'),
        'q_m': rmsnorm(seg(z, 'q_m').reshape(B, T, H_M, HD), qk_g[6]),
        'silu': (seg(z, 's_a'), seg(z, 's_b'), seg(z, 's_c'), seg(z, 's_m')),
    }


def compress(rows, w_cmp, pe_cmp, g_k):
    B, L = rows.shape[:2]
    n_ch = L // CMP_STRIDE
    ch = rows[:, :n_ch * CMP_STRIDE].reshape(B, n_ch, CMP_STRIDE, 2, G_A, HD)
    ch = jnp.moveaxis(ch, 3, 0)
    pe = pe_cmp[:, None, None]
    c = (jnp.einsum('kbncgd,kgcde->kbnge', ch[:, :, :-1] + pe[:, :, :, :CMP_STRIDE], w_cmp[:, :, :CMP_STRIDE])
         + jnp.einsum('kbncgd,kgcde->kbnge', ch[:, :, 1:] + pe[:, :, :, CMP_STRIDE:], w_cmp[:, :, CMP_STRIDE:]))
    return rmsnorm(c[0], g_k), c[1]


def cmp_attend(q_n, pos_q, kc, vc):
    B, T = q_n.shape[:2]
    qg = q_n.reshape(B, T, G_A, R_A, HD)
    s = jnp.einsum('btgrd,bngd->btgrn', qg, kc) * SCALE
    blk_end = jnp.arange(kc.shape[1], dtype=jnp.int32) * CMP_STRIDE + (CMP_LEN - 1)
    mask = blk_end[None, :] <= pos_q[:, None]
    p = masked_softmax(s, mask[None, :, None, None, :])
    o = jnp.einsum('btgrn,bngd->btgrd', p, vc)
    return o.reshape(B, T, H_A, HD), p


def select_blocks(p_cmp, pos_q, n_blk):
    pg = p_cmp.sum(axis=3)
    zero = jnp.zeros(pg.shape[:-1] + (1,), pg.dtype)
    chunk = jnp.concatenate([pg, zero], -1) + jnp.concatenate([zero, pg], -1)
    chunk = jnp.pad(chunk, ((0, 0), (0, 0), (0, 0), (0, n_blk * SEL_CHUNKS - chunk.shape[-1])))
    score = chunk.reshape(chunk.shape[:-1] + (n_blk, SEL_CHUNKS)).sum(-1)
    j = jnp.arange(n_blk, dtype=jnp.int32)[None, :]
    cur = (pos_q // SEL_LEN)[:, None]
    forced = (j == 0) | (j == cur) | (j == cur - 1)
    valid = j <= cur
    score = jnp.where(forced[None, :, None], FORCE, jnp.where(valid[None, :, None], score, NEG))
    a, b = score[..., None, :], score[..., :, None]
    first = j[0][None, :] < j[0][:, None]
    rank = jnp.sum((a > b) | ((a == b) & first), axis=-1)
    return (rank < min(N_SEL, n_blk)) & (score > 0.5 * NEG)


def sel_attend_dense(q_r, pos_q, k, v, blk):
    B, T = q_r.shape[:2]
    L = k.shape[1]
    kpos = jnp.arange(L, dtype=jnp.int32)
    mask = jnp.repeat(blk, SEL_LEN, axis=-1)[..., :L] & (kpos[None, :] <= pos_q[:, None])[None, :, None, :]
    qg = q_r.reshape(B, T, G_A, R_A, HD)
    s = jnp.einsum('btgrd,bsgd->btgrs', qg, k) * SCALE
    p = masked_softmax(s, mask[:, :, :, None, :])
    o = jnp.einsum('btgrs,bsgd->btgrd', p, v)
    return o.reshape(B, T, H_A, HD)


def win_attend(q_r, pos_q, kw, vw, pos_k):
    B, T = q_r.shape[:2]
    qg = q_r.reshape(B, T, G_A, R_A, HD)
    s = jnp.einsum('btgrd,bsgd->btgrs', qg, kw) * SCALE
    d = pos_q[:, None] - pos_k[None, :]
    mask = (d >= 0) & (d < WINDOW) & (pos_k[None, :] >= 0)
    p = masked_softmax(s, mask[None, :, None, None, :])
    o = jnp.einsum('btgrs,bsgd->btgrd', p, vw)
    return o.reshape(B, T, H_A, HD)


def nsa_combine(g_a, o_c, o_s, o_w):
    g = jax.nn.sigmoid(g_a)[..., None]
    o = g[:, :, 0] * o_c + g[:, :, 1] * o_s + g[:, :, 2] * o_w
    return o.reshape(o.shape[0], o.shape[1], W_A)


def dsa_select_bias(q_i, w_i, k_i, pos_q, pos_k, k_top):
    B, T = q_i.shape[:2]
    L = k_i.shape[1]
    logits = jnp.einsum('bthd,bsd->bths', q_i, k_i) * (D_I ** -0.5)
    score = jnp.einsum('bths,bth->bts', jax.nn.relu(logits), w_i) * (H_I ** -0.5)
    score = jnp.where((pos_k[None, :] <= pos_q[:, None])[None], score, NEG)
    lp = -(-L // LANES) * LANES
    score = jnp.pad(score, ((0, 0), (0, 0), (0, lp - L)), constant_values=NEG)
    return topk_bias(score.reshape(B * T, lp), k_top).reshape(B, T, lp)[:, :, :L]


def dsa_attend_dense(q_b, k, v, bias):
    B, T = q_b.shape[:2]
    s = jnp.einsum('bthd,bsd->bths', q_b, k) * SCALE
    p = masked_softmax(s, (bias > 0.5 * NEG)[:, :, None, :])
    o = jnp.einsum('bths,bsd->bthd', p, v)
    return o.reshape(B, T, W_B)


def pool_mix(u_ext, pos_q, w_pool, pool_scale):
    B, n_ext, _ = u_ext.shape
    T = n_ext - POOL_BUF
    cs = jnp.concatenate([jnp.zeros((B, 1, C_WIDTH), jnp.float32), jnp.cumsum(u_ext, axis=1)], axis=1)
    end = cs[:, POOL_BUF + 1:]
    means = []
    for gi, w in enumerate(POOL_WINDOWS):
        ch = slice(gi * C_GW, (gi + 1) * C_GW)
        start = cs[:, POOL_BUF + 1 - w: POOL_BUF + 1 - w + T, ch]
        cnt = jnp.minimum(pos_q + 1, w).astype(jnp.float32)[None, :, None]
        means.append((end[:, :, ch] - start) / cnt)
    d = jnp.concatenate(means, axis=-1) - u_ext[:, POOL_BUF:]
    y = jnp.einsum('btgc,gce->btge', d.reshape(B, T, C_GROUPS, C_GW), w_pool)
    return y.reshape(B, T, C_WIDTH) * pool_scale


def mem_kv(mem, mem_norm_g, w_mem_kv_b, g_k):
    B, M, _ = mem.shape
    h = rmsnorm_cast(mem.reshape(B * M, D_MODEL), mem_norm_g, _row_tile(B * M, 512))
    kv = matmul(h, w_mem_kv_b, _row_tile(B * M), 512).reshape(B, M, 2, H_M, HD)
    return jnp.stack([rmsnorm(kv[:, :, 0], g_k), kv[:, :, 1]], axis=2)


def mem_attend(q_m, kv):
    B, T = q_m.shape[:2]
    s = jnp.einsum('bthd,bmhd->bthm', q_m, kv[:, :, 0]) * SCALE
    p = jax.nn.softmax(s, axis=-1)
    o = jnp.einsum('bthm,bmhd->bthd', p, kv[:, :, 1])
    return o.reshape(B, T, W_M)


def gather_pages(pool, l, page_table):
    rows = pool[l, page_table]
    return rows.reshape((page_table.shape[0], -1) + rows.shape[3:])


def prompt_layer(x, mem, pos, norm_g, w_in_b, qk_g, w_cmp, pe_cmp, w_pool_b, pool_scale, mem_norm_g, w_mem_kv_b,
                 w_br_b, w_out_b):
    B, S, _ = x.shape
    x2 = x.reshape(B * S, D_MODEL)
    z = project(x2, norm_g, w_in_b)
    cos, sin = rope_tables(pos)
    sel_st, win_st, dsa_st, idx_st, ks, vs, kw, vw, kb, vb, ki = key_post(z, cos, sin, qk_g, B, S)
    kv_cmp = seg(z, 'kv_c').reshape(B, S, 2, G_A, HD)
    kc, vc = compress(kv_cmp, w_cmp, pe_cmp, qk_g[1])
    pad_c = lambda t: jnp.pad(t, ((0, 0), (0, LANES - t.shape[1]), (0, 0), (0, 0))).transpose(0, 2, 1, 3).astype(BF16)
    a_a = nsa_prompt_attention(z, cos, sin, qk_g, pad_c(kc), pad_c(vc), ks, vs, kw, vw, B, S)
    a_b = dsa_prompt_attention(z, cos, sin, qk_g, ki, kb, vb, B, S)
    a_p = pool_prompt(z, w_pool_b, pool_scale, B, S)
    kv_m = mem_kv(mem, mem_norm_g, w_mem_kv_b, qk_g[7])
    kvm_t = kv_m.transpose(2, 0, 3, 1, 4).astype(BF16)
    a_m = mem_prompt_attention(z, qk_g, kvm_t[0], kvm_t[1], B, S)
    y_mid = merge_branches((a_a, a_b, a_p, a_m), z, w_br_b)
    y = out_proj(x2, y_mid, w_out_b).reshape(B, S, D_MODEL)
    wb = min(WINDOW, S)
    u_c = seg(z, 'u_c').reshape(B, S, C_WIDTH)
    return y, (kv_cmp, sel_st.reshape(B, S, 2, G_A, HD), dsa_st.reshape(B, S, 2, HD), idx_st,
               win_st[:, S - wb:].reshape(B, wb, 2, G_A, HD), u_c[:, S - POOL_BUF:], kv_m)


def sample_layer(x, pos, l, page_table, cache_cmp, cache_sel, cache_dsa, cache_idx, win_buf, pool_buf, mem_cache,
                 norm_g, w_in_b, qk_g, w_cmp, pe_cmp, w_pool, pool_scale, w_br_b, w_out_b):
    B, T, _ = x.shape
    past = page_table.shape[1] * PAGE_SIZE
    L = past + T
    x2 = x.reshape(B * T, D_MODEL)
    z2 = project(x2, norm_g, w_in_b)
    f = front_sample(z2.reshape(B, T, N_FRONT), pos, qk_g)
    cmp_rows = gather_pages(cache_cmp, l, page_table)
    if (L // CMP_STRIDE) * CMP_STRIDE > past:
        cmp_rows = jnp.concatenate([cmp_rows, f['kv_cmp']], axis=1)
    kc, vc = compress(cmp_rows, w_cmp, pe_cmp, qk_g[1])
    o_c, p_c = cmp_attend(f['q_n'], pos, kc, vc)
    blk = select_blocks(p_c, pos, -(-L // SEL_LEN))
    sel_rows = jnp.concatenate([gather_pages(cache_sel, l, page_table), f['kv_sel']], axis=1)
    o_s = sel_attend_dense(f['q_r'], pos, sel_rows[:, :, 0], sel_rows[:, :, 1], blk)
    wb = win_buf.shape[1]
    win_rows = jnp.concatenate([win_buf, f['kv_win']], axis=1)
    pos_k = past - wb + jnp.arange(wb + T, dtype=jnp.int32)
    o_w = win_attend(f['q_r'], pos, win_rows[:, :, 0], win_rows[:, :, 1], pos_k)
    o_a = nsa_combine(f['g_a'], o_c, o_s, o_w)
    k_i_all = jnp.concatenate([gather_pages(cache_idx, l, page_table), f['k_i']], axis=1)
    bias_b = dsa_select_bias(f['q_i'], f['w_i'], k_i_all, pos, jnp.arange(L, dtype=jnp.int32),
                             min(DSA_TOPK, L // 4))
    kv_all = jnp.concatenate([gather_pages(cache_dsa, l, page_table), f['kv_b']], axis=1)
    o_b = dsa_attend_dense(f['q_b'], kv_all[:, :, 0], kv_all[:, :, 1], bias_b)
    u_ext = jnp.concatenate([pool_buf, f['u_c']], axis=1)
    o_p = pool_mix(u_ext, pos, w_pool, pool_scale)
    o_m = mem_attend(f['q_m'], mem_cache)
    acts = tuple((o * jax.nn.silu(s)).reshape(B * T, -1).astype(BF16)
                 for o, s in zip((o_a, o_b, o_p, o_m), f['silu']))
    y_mid = merge_branches(acts, z2, w_br_b)
    y = out_proj(x2, y_mid, w_out_b).reshape(B, T, D_MODEL)
    return y, (f['kv_cmp'], f['kv_sel'], f['kv_b'], f['k_i'], win_rows[:, T:], u_ext[:, T:])


def kernel(x_prompt, x_sample, mem_prompt, cache_cmp, cache_sel, cache_dsa, cache_idx, state_win, state_pool,
           cache_mem, page_table, norm_g, w_in, qk_g, w_cmp, pe_cmp, w_pool, pool_scale, mem_norm_g, w_mem_kv,
           w_br, w_out):
    past = page_table.shape[1] * PAGE_SIZE
    pos_p = jnp.arange(x_prompt.shape[1], dtype=jnp.int32)
    pos_s = past + jnp.arange(x_sample.shape[1], dtype=jnp.int32)
    xp, xs = x_prompt, x_sample
    st_p, st_s = [], []
    for l in range(DEPTH):
        w_in_b = prep_w_in(w_in[l])
        w_br_b = w_br[l].astype(BF16)
        w_out_b = w_out[l].astype(BF16)
        w_mem_kv_b = w_mem_kv[l].astype(BF16)
        xp, sp = prompt_layer(xp, mem_prompt, pos_p, norm_g[l], w_in_b, qk_g[l], w_cmp[l], pe_cmp[l],
                              w_pool[l].astype(BF16), pool_scale[l], mem_norm_g[l], w_mem_kv_b, w_br_b, w_out_b)
        xs, ss = sample_layer(xs, pos_s, l, page_table, cache_cmp, cache_sel, cache_dsa, cache_idx, state_win[l],
                              state_pool[l], cache_mem[l], norm_g[l], w_in_b, qk_g[l], w_cmp[l], pe_cmp[l],
                              w_pool[l], pool_scale[l], w_br_b, w_out_b)
        st_p.append(sp)
        st_s.append(ss)

    def stack(states, i):
        return jnp.stack([s[i] for s in states], axis=0)

    return (xp, xs,
            stack(st_p, 0), stack(st_s, 0),
            stack(st_p, 1), stack(st_s, 1),
            stack(st_p, 2), stack(st_s, 2),
            stack(st_p, 3), stack(st_s, 3),
            stack(st_p, 4), stack(st_s, 4),
            stack(st_p, 5), stack(st_s, 5),
            stack(st_p, 6))
```
